```python
import jax, jax.numpy as jnp
from jax import lax
import numpy as np

D_MODEL = 1024
BATCH = 4
SEQ = 4096
DEPTH = 4

CHUNK = 64
Q_BLOCK = 128
N_MIXERS = 3
EXPAND = 2
D_INNER = EXPAND * D_MODEL
N_HEADS = 16
HEAD_DIM = D_INNER // N_HEADS
MLA_Q_RANK = 256
MLA_KV_RANK = 128
MLA_NOPE_DIM = 128
MLA_ROPE_DIM = 64
MLA_V_DIM = D_INNER // N_HEADS
ROPE_BASE = 10000.0
FORGET_BIAS = 3.0
EPS = 1e-6
NEG = -1e30

kernel_name = 'hybrid_stickbreak_mla_forgetting_trunk'


def rmsnorm(x, g):
    x32 = x.astype(jnp.float32)
    y = x32 * lax.rsqrt(jnp.mean(x32 * x32, axis=-1, keepdims=True) + EPS)
    return (y * g.astype(jnp.float32)).astype(x.dtype)


def split_heads(t, n_heads):
    b, s, _ = t.shape
    return t.reshape(b, s, n_heads, -1).transpose(0, 2, 1, 3)


def merge_heads(t):
    b, h, s, d = t.shape
    return t.transpose(0, 2, 1, 3).reshape(b, s, h * d)


def sweep_query_blocks(block_fn, seq):
    outs = [block_fn(start, start + Q_BLOCK) for start in range(0, seq, Q_BLOCK)]
    return jnp.concatenate(outs, axis=2)


def rope(x, positions):
    r = x.shape[-1]
    inv_freq = ROPE_BASE ** (-jnp.arange(0, r, 2, dtype=jnp.float32) / r)
    ang = positions.astype(jnp.float32)[:, :, None, None] * inv_freq
    cos, sin = jnp.cos(ang), jnp.sin(ang)
    x32 = x.astype(jnp.float32)
    x1, x2 = x32[..., : r // 2], x32[..., r // 2:]
    return jnp.concatenate([x1 * cos - x2 * sin, x1 * sin + x2 * cos], axis=-1).astype(x.dtype)


def stick_breaking_mixer(h, w_in, w_out):
    b, s, _ = h.shape
    q, k, v, gate = jnp.split(h @ w_in, 4, axis=-1)
    q, k, v = split_heads(q, N_HEADS), split_heads(k, N_HEADS), split_heads(v, N_HEADS)
    scale = HEAD_DIM ** -0.5

    def block(start, end):
        z = jnp.einsum('bhqd,bhkd->bhqk', q[:, :, start:end], k[:, :, :end],
                       preferred_element_type=jnp.float32) * scale
        strict = jnp.arange(end)[None, :] < jnp.arange(start, end)[:, None]
        log_skip = jnp.where(strict, jax.nn.log_sigmoid(-z), 0.0)
        later = lax.cumsum(log_skip, axis=3, reverse=True) - log_skip
        w = jnp.where(strict, jnp.exp(jax.nn.log_sigmoid(z) + later), 0.0)
        return jnp.einsum('bhqk,bhkd->bhqd', w.astype(v.dtype), v[:, :, :end])

    o = merge_heads(sweep_query_blocks(block, s))
    return (o * jax.nn.silu(gate)) @ w_out


def mla_mixer(h, positions, w_in, q_norm, w_qb, kv_norm, w_kvb, w_out):
    b, s, _ = h.shape
    i1 = MLA_Q_RANK
    i2 = i1 + MLA_KV_RANK
    i3 = i2 + MLA_ROPE_DIM
    proj = h @ w_in
    q_lat, kv_lat, k_rope, gate = proj[..., :i1], proj[..., i1:i2], proj[..., i2:i3], proj[..., i3:]
    q = (rmsnorm(q_lat, q_norm) @ w_qb).reshape(b, s, N_HEADS, MLA_NOPE_DIM + MLA_ROPE_DIM)
    q_nope = q[..., :MLA_NOPE_DIM].transpose(0, 2, 1, 3)
    q_rope = rope(q[..., MLA_NOPE_DIM:], positions).transpose(0, 2, 1, 3)
    kv = (rmsnorm(kv_lat, kv_norm) @ w_kvb).reshape(b, s, N_HEADS, MLA_NOPE_DIM + MLA_V_DIM)
    k_nope = kv[..., :MLA_NOPE_DIM].transpose(0, 2, 1, 3)
    v = kv[..., MLA_NOPE_DIM:].transpose(0, 2, 1, 3)
    k_rope = rope(k_rope[:, :, None, :], positions)[:, :, 0, :]
    scale = (MLA_NOPE_DIM + MLA_ROPE_DIM) ** -0.5

    def block(start, end):
        z = (jnp.einsum('bhqd,bhkd->bhqk', q_nope[:, :, start:end], k_nope[:, :, :end],
                        preferred_element_type=jnp.float32)
             + jnp.einsum('bhqr,bkr->bhqk', q_rope[:, :, start:end], k_rope[:, :end],
                          preferred_element_type=jnp.float32)) * scale
        allowed = (jnp.arange(end)[None, :] // CHUNK) <= (jnp.arange(start, end)[:, None] // CHUNK)
        p = jax.nn.softmax(jnp.where(allowed, z, NEG), axis=-1)
        return jnp.einsum('bhqk,bhkd->bhqd', p.astype(v.dtype), v[:, :, :end])

    o = merge_heads(sweep_query_blocks(block, s))
    return (o * jax.nn.silu(gate)) @ w_out


def forgetting_mixer(h, w_in, b_f, w_out):
    b, s, _ = h.shape
    proj = h @ w_in
    q = split_heads(proj[..., :D_INNER], N_HEADS)
    k = split_heads(proj[..., D_INNER:2 * D_INNER], N_HEADS)
    v = split_heads(proj[..., 2 * D_INNER:3 * D_INNER], N_HEADS)
    gate = proj[..., 3 * D_INNER:4 * D_INNER]
    f_logit = proj[..., 4 * D_INNER:].astype(jnp.float32) + b_f.astype(jnp.float32)
    cum_log_f = lax.cumsum(jax.nn.log_sigmoid(f_logit), axis=1).transpose(0, 2, 1)
    scale = HEAD_DIM ** -0.5

    def block(start, end):
        z = jnp.einsum('bhqd,bhkd->bhqk', q[:, :, start:end], k[:, :, :end],
                       preferred_element_type=jnp.float32) * scale
        z = z + cum_log_f[:, :, start:end, None] - cum_log_f[:, :, None, :end]
        causal = jnp.arange(end)[None, :] <= jnp.arange(start, end)[:, None]
        p = jax.nn.softmax(jnp.where(causal, z, NEG), axis=-1)
        return jnp.einsum('bhqk,bhkd->bhqd', p.astype(v.dtype), v[:, :, :end])

    o = merge_heads(sweep_query_blocks(block, s))
    return (o * jax.nn.silu(gate)) @ w_out


def setup_inputs(seed: int = 0) -> dict:
    key = jax.random.key(seed)
    ks = iter(jax.random.split(key, 32))

    def w(shape):
        return jax.random.normal(next(ks), shape, jnp.float32) * shape[0] ** -0.5

    def gain(n):
        return 1.0 + 0.02 * jax.random.normal(next(ks), (n,), jnp.float32)

    x = jax.random.normal(next(ks), (BATCH, SEQ, D_MODEL), jnp.float32)
    offsets = jax.random.randint(next(ks), (BATCH,), 0, 64) * CHUNK
    positions = (offsets[:, None] + jnp.arange(SEQ)[None, :]).astype(jnp.int32)
    mla_in = MLA_Q_RANK + MLA_KV_RANK + MLA_ROPE_DIM + D_INNER
    return {
        'x': x,
        'positions': positions,
        'ln0': gain(D_MODEL),
        'w_in0': w((D_MODEL, 4 * D_INNER)),
        'w_out0': w((D_INNER, D_MODEL)),
        'ln1': gain(D_MODEL),
        'w_in1': w((D_MODEL, mla_in)),
        'q_norm1': gain(MLA_Q_RANK),
        'w_qb1': w((MLA_Q_RANK, N_HEADS * (MLA_NOPE_DIM + MLA_ROPE_DIM))),
        'kv_norm1': gain(MLA_KV_RANK),
        'w_kvb1': w((MLA_KV_RANK, N_HEADS * (MLA_NOPE_DIM + MLA_V_DIM))),
        'w_out1': w((D_INNER, D_MODEL)),
        'ln2': gain(D_MODEL),
        'w_in2': w((D_MODEL, 4 * D_INNER + N_HEADS)),
        'b_f2': FORGET_BIAS + 0.1 * jax.random.normal(next(ks), (N_HEADS,), jnp.float32),
        'w_out2': w((D_INNER, D_MODEL)),
        'ln3': gain(D_MODEL),
        'w_in3': w((D_MODEL, 4 * D_INNER)),
        'w_out3': w((D_INNER, D_MODEL)),
        'final_norm': gain(D_MODEL),
    }


def reference(x, positions, ln0, w_in0, w_out0, ln1, w_in1, q_norm1, w_qb1, kv_norm1, w_kvb1,
              w_out1, ln2, w_in2, b_f2, w_out2, ln3, w_in3, w_out3, final_norm):
    layer_params = [
        (ln0, (w_in0, w_out0)),
        (ln1, (w_in1, q_norm1, w_qb1, kv_norm1, w_kvb1, w_out1)),
        (ln2, (w_in2, b_f2, w_out2)),
        (ln3, (w_in3, w_out3)),
    ]
    for i in range(DEPTH):
        ln, p = layer_params[i]
        h = rmsnorm(x, ln)
        kind = i % N_MIXERS
        if kind == 0:
            y = stick_breaking_mixer(h, *p)
        elif kind == 1:
            y = mla_mixer(h, positions, *p)
        else:
            y = forgetting_mixer(h, *p)
        x = x + y
    return rmsnorm(x, final_norm)
```

```python
import functools

import jax
import jax.numpy as jnp
from jax import lax
from jax.experimental import pallas as pl
from jax.experimental.pallas import tpu as pltpu

D_MODEL = 1024
N_HEADS = 16
HEAD_DIM = 128
D_INNER = N_HEADS * HEAD_DIM
CHUNK = 64
MLA_Q_RANK = 256
MLA_KV_RANK = 128
MLA_NOPE_DIM = 128
MLA_ROPE_DIM = 64
ROPE_BASE = 10000.0
EPS = 1e-6
NEG = -1e30

LANES = 128
VMEM_LIMIT_BYTES = 56 * 1024 * 1024

F32 = jnp.float32
BF16 = jnp.bfloat16


def _params(*semantics):
    return pltpu.CompilerParams(dimension_semantics=semantics,
                                vmem_limit_bytes=VMEM_LIMIT_BYTES)


def _resident(shape):
    zeros = (0,) * len(shape)
    return pl.BlockSpec(shape, lambda *_: zeros, pipeline_mode=pl.Buffered(1))


def _rmsnorm(x, g):
    return x * lax.rsqrt(jnp.mean(x * x, axis=-1, keepdims=True) + EPS) * g


def _dot(a, b):
    return jnp.dot(a, b, preferred_element_type=F32)


def _dot_nt(a, b):
    return lax.dot_general(a, b, (((1,), (1,)), ((), ())), preferred_element_type=F32)


def _split_dot(a, ones_matrix, terms):
    out = None
    rest = a
    for t in range(terms):
        part = rest.astype(BF16)
        d = _dot(part, ones_matrix)
        out = d if out is None else out + d
        if t + 1 < terms:
            rest = rest - part.astype(F32)
    return out


def _silu(g):
    return g * (1.0 / (1.0 + jnp.exp(-g)))


def _proj_kernel(x_ref, g_ref, w_ref, *rest, n_bf16, n_gate, with_forget):
    if with_forget:
        wf_ref, bf_ref, qkv_ref, gate_ref, fl_ref = rest
    else:
        qkv_ref, gate_ref = rest
    h = _rmsnorm(x_ref[...], g_ref[...]).astype(BF16)
    step = 8
    for c0 in range(0, n_bf16 + n_gate, step):
        acc = _dot(h, w_ref[:, c0 * LANES:(c0 + step) * LANES])
        for c in range(step):
            piece = acc[:, c * LANES:(c + 1) * LANES]
            if c0 + c < n_bf16:
                qkv_ref[c0 + c] = piece.astype(BF16)
            else:
                gate_ref[c0 + c - n_bf16] = piece
    if with_forget:
        fl_ref[0] = _dot_nt(wf_ref[...], h) + bf_ref[...]


def _norm_proj(x2d, g, w, tokens_per_batch, wf=None, bf=None):
    m, d = x2d.shape
    n = w.shape[1]
    n_groups = n // LANES
    n_gate = N_HEADS
    n_bf16 = n_groups - n_gate
    tm = min(256, tokens_per_batch)
    with_forget = wf is not None
    in_specs = [pl.BlockSpec((tm, d), lambda i: (i, 0)), _resident((1, d)), _resident((d, n))]
    args = [x2d, g.reshape(1, d), w]
    out_shape = [jax.ShapeDtypeStruct((n_bf16, m, LANES), BF16),
                 jax.ShapeDtypeStruct((n_gate, m, LANES), F32)]
    out_specs = [pl.BlockSpec((n_bf16, tm, LANES), lambda i: (0, i, 0)),
                 pl.BlockSpec((n_gate, tm, LANES), lambda i: (0, i, 0))]
    if with_forget:
        per_batch = tokens_per_batch // tm
        in_specs += [_resident((N_HEADS, d)), _resident((N_HEADS, 1))]
        args += [wf, bf.reshape(N_HEADS, 1)]
        out_shape.append(jax.ShapeDtypeStruct((m // tokens_per_batch, N_HEADS, tokens_per_batch), F32))
        out_specs.append(pl.BlockSpec((1, N_HEADS, tm), lambda i: (i // per_batch, 0, i % per_batch)))
    return pl.pallas_call(
        functools.partial(_proj_kernel, n_bf16=n_bf16, n_gate=n_gate, with_forget=with_forget),
        grid=(m // tm,),
        in_specs=in_specs, out_specs=out_specs, out_shape=out_shape,
        compiler_params=_params("parallel"),
        name="norm_proj_forget" if with_forget else "norm_proj",
    )(*args)


def _out_kernel(o_ref, w_ref, x_ref, *rest, final):
    if final:
        g_ref, y_ref = rest
    else:
        (y_ref,) = rest
    o = jnp.concatenate([o_ref[hh] for hh in range(N_HEADS)], axis=1)
    y = x_ref[...] + _dot(o, w_ref[...])
    if final:
        y = _rmsnorm(y, g_ref[...])
    y_ref[...] = y


def _out_proj(o, w_out, x2d, final_g=None):
    m, d = x2d.shape
    tm = min(512, m)
    final = final_g is not None
    in_specs = [pl.BlockSpec((N_HEADS, tm, LANES), lambda i: (0, i, 0)),
                _resident((D_INNER, d)),
                pl.BlockSpec((tm, d), lambda i: (i, 0))]
    args = [o, w_out, x2d]
    if final:
        in_specs.append(_resident((1, d)))
        args.append(final_g.reshape(1, d))
    return pl.pallas_call(
        functools.partial(_out_kernel, final=final),
        grid=(m // tm,),
        in_specs=in_specs,
        out_specs=pl.BlockSpec((tm, d), lambda i: (i, 0)),
        out_shape=jax.ShapeDtypeStruct((m, d), F32),
        compiler_params=_params("parallel"),
        name="out_proj_final" if final else "out_proj",
    )(*args)


def _stick_kernel(q_ref, k_ref, v_ref, g_ref, o_ref, *, t, scale):
    i = pl.program_id(2)
    q = q_ref[0]
    row = lax.broadcasted_iota(jnp.int32, (t, t), 0)
    col = lax.broadcasted_iota(jnp.int32, (t, t), 1)
    later_sum = (row > col).astype(BF16)
    strict = col < row

    def block(kj, c, acc, diagonal):
        start = pl.multiple_of(kj * t, t)
        kb = k_ref[0, pl.ds(start, t), :]
        vb = v_ref[0, pl.ds(start, t), :]
        z = _dot_nt(q, kb) * scale
        sp = jnp.log1p(jnp.exp(-jnp.abs(z)))
        log_beta = jnp.minimum(z, 0.0) - sp
        log_skip = log_beta - z
        if diagonal:
            log_skip = jnp.where(strict, log_skip, 0.0)
        later = _split_dot(log_skip, later_sum, 2)
        w = jnp.exp(log_beta + later + c)
        if diagonal:
            w = jnp.where(strict, w, 0.0)
        acc = acc + _dot(w.astype(BF16), vb)
        c = c + jnp.sum(log_skip, axis=1, keepdims=True)
        return c, acc

    c, acc = block(i, jnp.zeros((t, 1), F32), jnp.zeros((t, HEAD_DIM), F32), True)

    def body(n, carry):
        return block(i - 1 - n, *carry, False)

    c, acc = lax.fori_loop(0, i, body, (c, acc))
    o_ref[0] = (acc * _silu(g_ref[0])).astype(BF16)


def _stick_attention(qkv, gate, batch, seq):
    t = min(256, seq)
    nq = seq // t
    m = batch * seq
    h = N_HEADS
    return pl.pallas_call(
        functools.partial(_stick_kernel, t=t, scale=HEAD_DIM ** -0.5),
        grid=(h, batch, nq),
        in_specs=[
            pl.BlockSpec((1, t, LANES), lambda hh, b, i: (hh, b * nq + i, 0)),
            pl.BlockSpec((1, seq, LANES), lambda hh, b, i: (h + hh, b, 0)),
            pl.BlockSpec((1, seq, LANES), lambda hh, b, i: (2 * h + hh, b, 0)),
            pl.BlockSpec((1, t, LANES), lambda hh, b, i: (hh, b * nq + i, 0)),
        ],
        out_specs=pl.BlockSpec((1, t, LANES), lambda hh, b, i: (hh, b * nq + i, 0)),
        out_shape=jax.ShapeDtypeStruct((h, m, LANES), BF16),
        compiler_params=_params("parallel", "parallel", "arbitrary"),
        name="stick_attention",
    )(qkv, qkv, qkv, gate)


def _softmax_kernel(q_ref, k_ref, v_ref, g_ref, *rest, t, scale, forget):
    if forget:
        cf_ref, o_ref = rest
    else:
        (o_ref,) = rest
    i = pl.program_id(2)
    q = q_ref[0]
    row = lax.broadcasted_iota(jnp.int32, (t, t), 0)
    col = lax.broadcasted_iota(jnp.int32, (t, t), 1)
    if forget:
        allowed = col <= row
        cf_q = cf_ref[0, :, pl.ds(pl.multiple_of(i * t, t), t)]
        cf_q_col = jnp.sum(jnp.where(row == col, jnp.broadcast_to(cf_q, (t, t)), 0.0),
                           axis=1, keepdims=True)
    else:
        allowed = (col // CHUNK) <= (row // CHUNK)

    def block(kj, m_run, l_run, acc, diagonal):
        start = pl.multiple_of(kj * t, t)
        kb = k_ref[0, pl.ds(start, t), :]
        vb = v_ref[0, pl.ds(start, t), :]
        s = _dot_nt(q, kb) * scale
        if forget:
            s = s + cf_q_col - cf_ref[0, :, pl.ds(start, t)]
        if diagonal:
            s = jnp.where(allowed, s, NEG)
        m_new = jnp.maximum(m_run, jnp.max(s, axis=1, keepdims=True))
        alpha = jnp.exp(m_run - m_new)
        p = jnp.exp(s - m_new)
        l_run = alpha * l_run + jnp.sum(p, axis=1, keepdims=True)
        acc = alpha * acc + _dot(p.astype(BF16), vb)
        return m_new, l_run, acc

    init = (jnp.full((t, 1), NEG, F32), jnp.zeros((t, 1), F32), jnp.zeros((t, HEAD_DIM), F32))
    carry = block(i, *init, True)

    def body(n, carry):
        return block(i - 1 - n, *carry, False)

    _, l_run, acc = lax.fori_loop(0, i, body, carry)
    o_ref[0] = (acc / l_run * _silu(g_ref[0])).astype(BF16)


def _softmax_attention(q_arr, q_off, k_arr, k_off, v_arr, v_off, gate, batch, seq, scale, cf=None):
    t = min(256, seq)
    nq = seq // t
    m = batch * seq
    h = N_HEADS
    qk_width = q_arr.shape[-1]
    forget = cf is not None
    in_specs = [
        pl.BlockSpec((1, t, qk_width), lambda hh, b, i: (q_off + hh, b * nq + i, 0)),
        pl.BlockSpec((1, seq, qk_width), lambda hh, b, i: (k_off + hh, b, 0)),
        pl.BlockSpec((1, seq, LANES), lambda hh, b, i: (v_off + hh, b, 0)),
        pl.BlockSpec((1, t, LANES), lambda hh, b, i: (hh, b * nq + i, 0)),
    ]
    args = [q_arr, k_arr, v_arr, gate]
    if forget:
        in_specs.append(pl.BlockSpec((1, 1, seq), lambda hh, b, i: (b * h + hh, 0, 0)))
        args.append(cf.reshape(batch * h, 1, seq))
    return pl.pallas_call(
        functools.partial(_softmax_kernel, t=t, scale=scale, forget=forget),
        grid=(h, batch, nq),
        in_specs=in_specs,
        out_specs=pl.BlockSpec((1, t, LANES), lambda hh, b, i: (hh, b * nq + i, 0)),
        out_shape=jax.ShapeDtypeStruct((h, m, LANES), BF16),
        compiler_params=_params("parallel", "parallel", "arbitrary"),
        name="forget_attention" if forget else "mla_attention",
    )(*args)


def _forget_cumsum_kernel(fl_ref, cf_ref, *, seq, width):
    row = lax.broadcasted_iota(jnp.int32, (width, width), 0)
    col = lax.broadcasted_iota(jnp.int32, (width, width), 1)
    prefix = (row <= col).astype(BF16)
    carry = jnp.zeros((N_HEADS, 1), F32)
    for c0 in range(0, seq, width):
        f = fl_ref[0, :, c0:c0 + width]
        log_f = jnp.minimum(f, 0.0) - jnp.log1p(jnp.exp(-jnp.abs(f)))
        cum = _split_dot(log_f, prefix, 3) + carry
        cf_ref[0, :, c0:c0 + width] = cum
        carry = cum[:, width - 1:width]


def _forget_cumsum(f_logit):
    batch, h, seq = f_logit.shape
    width = min(512, seq)
    return pl.pallas_call(
        functools.partial(_forget_cumsum_kernel, seq=seq, width=width),
        grid=(batch,),
        in_specs=[pl.BlockSpec((1, h, seq), lambda b: (b, 0, 0))],
        out_specs=pl.BlockSpec((1, h, seq), lambda b: (b, 0, 0)),
        out_shape=jax.ShapeDtypeStruct((batch, h, seq), F32),
        compiler_params=_params("parallel"),
        name="forget_cumsum",
    )(f_logit)


def _mla_proj_kernel(x_ref, pos_ref, inv_ref, g_ref, w1_ref, qn_ref, wq_ref, kvn_ref, wkv_ref,
                     qcat_ref, kcat_ref, v_ref, gate_ref):
    h = _rmsnorm(x_ref[...], g_ref[...]).astype(BF16)
    lat = _dot(h, w1_ref[:, :5 * LANES])
    q_lat = lat[:, :MLA_Q_RANK]
    kv_lat = lat[:, MLA_Q_RANK:MLA_Q_RANK + MLA_KV_RANK]
    k_a = lat[:, 3 * LANES:4 * LANES]
    k_b = lat[:, 4 * LANES:5 * LANES]

    lane = lax.broadcasted_iota(jnp.int32, (1, LANES), 1)
    half = MLA_ROPE_DIM // 2
    ang = pos_ref[...] * inv_ref[...]
    cos = jnp.cos(ang)
    sin = jnp.where((lane // half) % 2 == 0, -1.0, 1.0) * jnp.sin(ang)
    k_rope = (k_a * cos + k_b * sin).astype(BF16)

    kvn = _rmsnorm(kv_lat, kvn_ref[...]).astype(BF16)
    for c0 in range(0, 2 * N_HEADS, 8):
        kv = _dot(kvn, wkv_ref[:, c0 * LANES:(c0 + 8) * LANES])
        for c in range(8):
            piece = kv[:, c * LANES:(c + 1) * LANES].astype(BF16)
            if c0 + c < N_HEADS:
                kcat_ref[c0 + c] = jnp.concatenate([piece, k_rope], axis=1)
            else:
                v_ref[c0 + c - N_HEADS] = piece

    qn = _rmsnorm(q_lat, qn_ref[...]).astype(BF16)
    n_pairs = N_HEADS // 2
    rope_a = _dot(qn, wq_ref[:, N_HEADS * LANES:(N_HEADS + n_pairs) * LANES])
    rope_b = _dot(qn, wq_ref[:, (N_HEADS + n_pairs) * LANES:])
    first_head = lane < MLA_ROPE_DIM
    for c0 in range(0, N_HEADS, 8):
        nope = _dot(qn, wq_ref[:, c0 * LANES:(c0 + 8) * LANES])
        for c in range(8):
            hh = c0 + c
            pair = hh // 2
            sl = slice(pair * LANES, (pair + 1) * LANES)
            q_rope = rope_a[:, sl] * cos + rope_b[:, sl] * sin
            mine = first_head if hh % 2 == 0 else jnp.logical_not(first_head)
            q_rope = jnp.where(mine, q_rope, 0.0)
            qcat_ref[hh] = jnp.concatenate(
                [nope[:, c * LANES:(c + 1) * LANES].astype(BF16), q_rope.astype(BF16)], axis=1)

    for c0 in range(0, N_HEADS, 8):
        gt = _dot(h, w1_ref[:, (5 + c0) * LANES:(5 + c0 + 8) * LANES])
        for c in range(8):
            gate_ref[c0 + c] = gt[:, c * LANES:(c + 1) * LANES]


def _mla_weights(w_in, w_qb, w_kvb):
    i1 = MLA_Q_RANK
    i2 = i1 + MLA_KV_RANK
    i3 = i2 + MLA_ROPE_DIM
    half = MLA_ROPE_DIM // 2
    wk1, wk2 = w_in[:, i2:i2 + half], w_in[:, i2 + half:i3]
    w1 = jnp.concatenate([w_in[:, :i2], wk1, wk2, wk1, wk2, wk2, wk1, wk2, wk1, w_in[:, i3:]], axis=1)
    wq = w_qb.reshape(MLA_Q_RANK, N_HEADS, MLA_NOPE_DIM + MLA_ROPE_DIM)
    nope = wq[:, :, :MLA_NOPE_DIM].reshape(MLA_Q_RANK, -1)
    r1 = wq[:, :, MLA_NOPE_DIM:MLA_NOPE_DIM + half]
    r2 = wq[:, :, MLA_NOPE_DIM + half:]
    rope_a = jnp.concatenate([r1, r2], axis=2).reshape(MLA_Q_RANK, -1)
    rope_b = jnp.concatenate([r2, r1], axis=2).reshape(MLA_Q_RANK, -1)
    wq2 = jnp.concatenate([nope, rope_a, rope_b], axis=1)
    wkv = w_kvb.reshape(MLA_KV_RANK, N_HEADS, 2 * LANES)
    wkv2 = jnp.concatenate([wkv[:, :, :MLA_NOPE_DIM].reshape(MLA_KV_RANK, -1),
                            wkv[:, :, MLA_NOPE_DIM:].reshape(MLA_KV_RANK, -1)], axis=1)
    return w1.astype(BF16), wq2.astype(BF16), wkv2.astype(BF16)


def _mla_proj(x2d, positions, ln, w_in, q_norm, w_qb, kv_norm, w_kvb, tokens_per_batch):
    m, d = x2d.shape
    tm = min(256, tokens_per_batch)
    w1, wq2, wkv2 = _mla_weights(w_in, w_qb, w_kvb)
    half = MLA_ROPE_DIM // 2
    inv_freq = ROPE_BASE ** (-jnp.arange(0, MLA_ROPE_DIM, 2, dtype=F32) / MLA_ROPE_DIM)
    inv4 = jnp.tile(inv_freq, LANES // half).reshape(1, LANES)
    pos = positions.astype(F32).reshape(m, 1)
    h = N_HEADS
    head_major = lambda width, dtype: jax.ShapeDtypeStruct((h, m, width), dtype)
    head_block = lambda width: pl.BlockSpec((h, tm, width), lambda i: (0, i, 0))
    return pl.pallas_call(
        _mla_proj_kernel,
        grid=(m // tm,),
        in_specs=[pl.BlockSpec((tm, d), lambda i: (i, 0)),
                  pl.BlockSpec((tm, 1), lambda i: (i, 0)),
                  _resident((1, LANES)), _resident((1, d)), _resident(w1.shape),
                  _resident((1, MLA_Q_RANK)), _resident(wq2.shape),
                  _resident((1, MLA_KV_RANK)), _resident(wkv2.shape)],
        out_specs=[head_block(2 * LANES), head_block(2 * LANES), head_block(LANES), head_block(LANES)],
        out_shape=[head_major(2 * LANES, BF16), head_major(2 * LANES, BF16),
                   head_major(LANES, BF16), head_major(LANES, F32)],
        compiler_params=_params("parallel"),
        name="mla_proj",
    )(x2d, pos, inv4, ln.reshape(1, d), w1, q_norm.reshape(1, -1), wq2, kv_norm.reshape(1, -1), wkv2)


def kernel(x, positions, ln0, w_in0, w_out0, ln1, w_in1, q_norm1, w_qb1, kv_norm1, w_kvb1, w_out1,
           ln2, w_in2, b_f2, w_out2, ln3, w_in3, w_out3, final_norm):
    batch, seq, d = x.shape
    h = N_HEADS
    x2d = x.reshape(batch * seq, d)

    qkv, gate = _norm_proj(x2d, ln0, w_in0.astype(BF16), seq)
    o = _stick_attention(qkv, gate, batch, seq)
    x2d = _out_proj(o, w_out0.astype(BF16), x2d)

    qcat, kcat, v, gate = _mla_proj(x2d, positions, ln1, w_in1, q_norm1, w_qb1, kv_norm1, w_kvb1, seq)
    o = _softmax_attention(qcat, 0, kcat, 0, v, 0, gate, batch, seq,
                           (MLA_NOPE_DIM + MLA_ROPE_DIM) ** -0.5)
    x2d = _out_proj(o, w_out1.astype(BF16), x2d)

    w2 = w_in2[:, :4 * D_INNER].astype(BF16)
    wf = w_in2[:, 4 * D_INNER:].T.astype(BF16)
    qkv, gate, f_logit = _norm_proj(x2d, ln2, w2, seq, wf=wf, bf=b_f2)
    cf = _forget_cumsum(f_logit)
    o = _softmax_attention(qkv, 0, qkv, h, qkv, 2 * h, gate, batch, seq, HEAD_DIM ** -0.5, cf=cf)
    x2d = _out_proj(o, w_out2.astype(BF16), x2d)

    qkv, gate = _norm_proj(x2d, ln3, w_in3.astype(BF16), seq)
    o = _stick_attention(qkv, gate, batch, seq)
    x2d = _out_proj(o, w_out3.astype(BF16), x2d, final_g=final_norm)
    return x2d.reshape(batch, seq, d)
```

```python
import functools

import jax
import jax.numpy as jnp
from jax import lax
from jax.experimental import pallas as pl
from jax.experimental.pallas import tpu as pltpu

D_MODEL = 1024
N_HEADS = 16
HEAD_DIM = 128
D_INNER = N_HEADS * HEAD_DIM
CHUNK = 64
MLA_Q_RANK = 256
MLA_KV_RANK = 128
MLA_NOPE_DIM = 128
MLA_ROPE_DIM = 64
ROPE_BASE = 10000.0
EPS = 1e-6
NEG = -1e30
LOG2E = 1.4426950408889634
UNDERFLOW_LOG2 = 150.0

LANES = 128
VMEM_LIMIT_BYTES = 56 * 1024 * 1024

F32 = jnp.float32
BF16 = jnp.bfloat16


def _params(*semantics):
    return pltpu.CompilerParams(dimension_semantics=semantics,
                                vmem_limit_bytes=VMEM_LIMIT_BYTES)


def _resident(shape):
    zeros = (0,) * len(shape)
    return pl.BlockSpec(shape, lambda *_: zeros, pipeline_mode=pl.Buffered(1))


def _rmsnorm(x, g):
    return x * lax.rsqrt(jnp.mean(x * x, axis=-1, keepdims=True) + EPS) * g


def _dot(a, b):
    return jnp.dot(a, b, preferred_element_type=F32)


def _dot_nt(a, b):
    return lax.dot_general(a, b, (((1,), (1,)), ((), ())), preferred_element_type=F32)


def _split_dot(a, ones_matrix, terms):
    out = None
    rest = a
    for t in range(terms):
        part = rest.astype(BF16)
        d = _dot(part, ones_matrix)
        out = d if out is None else out + d
        if t + 1 < terms:
            rest = rest - part.astype(F32)
    return out


def _silu(g):
    return g * (1.0 / (1.0 + jnp.exp(-g)))


def _proj_kernel(x_ref, g_ref, w_ref, *rest, n_bf16, n_gate, with_forget):
    if with_forget:
        wf_ref, bf_ref, qkv_ref, gate_ref, fl_ref = rest
    else:
        qkv_ref, gate_ref = rest
    h = _rmsnorm(x_ref[...], g_ref[...]).astype(BF16)
    step = 8
    for c0 in range(0, n_bf16 + n_gate, step):
        acc = _dot(h, w_ref[:, c0 * LANES:(c0 + step) * LANES])
        for c in range(step):
            piece = acc[:, c * LANES:(c + 1) * LANES]
            if c0 + c < N_HEADS:
                qkv_ref[c0 + c] = (piece * (HEAD_DIM ** -0.5 * LOG2E)).astype(BF16)
            elif c0 + c < n_bf16:
                qkv_ref[c0 + c] = piece.astype(BF16)
            else:
                gate_ref[c0 + c - n_bf16] = piece
    if with_forget:
        fl_ref[0] = _dot_nt(wf_ref[...], h) + bf_ref[...]


def _norm_proj(x2d, g, w, tokens_per_batch, wf=None, bf=None):
    m, d = x2d.shape
    n = w.shape[1]
    n_groups = n // LANES
    n_gate = N_HEADS
    n_bf16 = n_groups - n_gate
    tm = min(256, tokens_per_batch)
    with_forget = wf is not None
    in_specs = [pl.BlockSpec((tm, d), lambda i: (i, 0)), _resident((1, d)), _resident((d, n))]
    args = [x2d, g.reshape(1, d), w]
    out_shape = [jax.ShapeDtypeStruct((n_bf16, m, LANES), BF16),
                 jax.ShapeDtypeStruct((n_gate, m, LANES), F32)]
    out_specs = [pl.BlockSpec((n_bf16, tm, LANES), lambda i: (0, i, 0)),
                 pl.BlockSpec((n_gate, tm, LANES), lambda i: (0, i, 0))]
    if with_forget:
        per_batch = tokens_per_batch // tm
        in_specs += [_resident((N_HEADS, d)), _resident((N_HEADS, 1))]
        args += [wf, bf.reshape(N_HEADS, 1)]
        out_shape.append(jax.ShapeDtypeStruct((m // tokens_per_batch, N_HEADS, tokens_per_batch), F32))
        out_specs.append(pl.BlockSpec((1, N_HEADS, tm), lambda i: (i // per_batch, 0, i % per_batch)))
    return pl.pallas_call(
        functools.partial(_proj_kernel, n_bf16=n_bf16, n_gate=n_gate, with_forget=with_forget),
        grid=(m // tm,),
        in_specs=in_specs, out_specs=out_specs, out_shape=out_shape,
        compiler_params=_params("parallel"),
        name="norm_proj_forget" if with_forget else "norm_proj",
    )(*args)


def _out_kernel(o_ref, w_ref, x_ref, *rest, final):
    if final:
        g_ref, y_ref = rest
    else:
        (y_ref,) = rest
    o = jnp.concatenate([o_ref[hh] for hh in range(N_HEADS)], axis=1)
    y = x_ref[...] + _dot(o, w_ref[...])
    if final:
        y = _rmsnorm(y, g_ref[...])
    y_ref[...] = y


def _out_proj(o, w_out, x2d, final_g=None):
    m, d = x2d.shape
    tm = min(512, m)
    final = final_g is not None
    in_specs = [pl.BlockSpec((N_HEADS, tm, LANES), lambda i: (0, i, 0)),
                _resident((D_INNER, d)),
                pl.BlockSpec((tm, d), lambda i: (i, 0))]
    args = [o, w_out, x2d]
    if final:
        in_specs.append(_resident((1, d)))
        args.append(final_g.reshape(1, d))
    return pl.pallas_call(
        functools.partial(_out_kernel, final=final),
        grid=(m // tm,),
        in_specs=in_specs,
        out_specs=pl.BlockSpec((tm, d), lambda i: (i, 0)),
        out_shape=jax.ShapeDtypeStruct((m, d), F32),
        compiler_params=_params("parallel"),
        name="out_proj_final" if final else "out_proj",
    )(*args)


def _stick_kernel(q_ref, k_ref, v_ref, g_ref, o_ref, *, t):
    i = pl.program_id(2)
    q = q_ref[0]
    row = lax.broadcasted_iota(jnp.int32, (t, t), 0)
    col = lax.broadcasted_iota(jnp.int32, (t, t), 1)
    from_here = (row >= col).astype(BF16)
    suffix_sum = jnp.concatenate([from_here, from_here], axis=0)
    strict = col < row

    def keys(kj):
        return pl.ds(pl.multiple_of(kj * t, t), t)

    def skips(kj):
        z = _dot_nt(q, k_ref[0, keys(kj), :])
        u = jnp.maximum(z, 0.0) + jnp.log2(1.0 + jnp.exp2(-jnp.abs(z)))
        return z, u

    def weights(z, u, c):
        hi = u.astype(BF16)
        lo = (u - hi.astype(F32)).astype(BF16)
        later = _dot(jnp.concatenate([hi, lo], axis=1), suffix_sum)
        return jnp.exp2(z - later + c)

    def row_sum(u):
        return jnp.sum(u, axis=1, keepdims=True)

    z, u = skips(i)
    u = jnp.where(strict, u, 0.0)
    w = jnp.where(strict, weights(z, u, 0.0), 0.0)
    acc = _dot(w.astype(BF16), v_ref[0, keys(i), :])
    c = -row_sum(u)

    prev = jnp.maximum(i - 1, 0)
    z, u = skips(prev)
    w = weights(z, u, c)
    vb = v_ref[0, keys(prev), :]
    acc = acc + _dot(w.astype(BF16), jnp.where(i > 0, vb, jnp.zeros_like(vb)))
    c = c - row_sum(u)

    def more(carry):
        kj, c, _ = carry
        return jnp.logical_and(kj >= 0, jnp.max(c) > -UNDERFLOW_LOG2)

    def body(carry):
        kj, c, acc = carry
        z, u = skips(kj)
        w = weights(z, u, c)
        acc = acc + _dot(w.astype(BF16), v_ref[0, keys(kj), :])
        return kj - 1, c - row_sum(u), acc

    _, _, acc = lax.while_loop(more, body, (i - 2, c, acc))
    o_ref[0] = (acc * _silu(g_ref[0])).astype(BF16)


def _stick_attention(qkv, gate, batch, seq):
    t = min(256, seq)
    nq = seq // t
    m = batch * seq
    h = N_HEADS
    return pl.pallas_call(
        functools.partial(_stick_kernel, t=t),
        grid=(h, batch, nq),
        in_specs=[
            pl.BlockSpec((1, t, LANES), lambda hh, b, i: (hh, b * nq + i, 0)),
            pl.BlockSpec((1, seq, LANES), lambda hh, b, i: (h + hh, b, 0)),
            pl.BlockSpec((1, seq, LANES), lambda hh, b, i: (2 * h + hh, b, 0)),
            pl.BlockSpec((1, t, LANES), lambda hh, b, i: (hh, b * nq + i, 0)),
        ],
        out_specs=pl.BlockSpec((1, t, LANES), lambda hh, b, i: (hh, b * nq + i, 0)),
        out_shape=jax.ShapeDtypeStruct((h, m, LANES), BF16),
        compiler_params=_params("parallel", "parallel", "arbitrary"),
        name="stick_attention",
    )(qkv, qkv, qkv, gate)


def _softmax_kernel(q_ref, k_ref, v_ref, g_ref, *rest, t, forget):
    if forget:
        cf_ref, o_ref = rest
    else:
        (o_ref,) = rest
    i = pl.program_id(2)
    q = q_ref[0]
    row = lax.broadcasted_iota(jnp.int32, (t, t), 0)
    col = lax.broadcasted_iota(jnp.int32, (t, t), 1)
    if forget:
        allowed = col <= row
        cf_q = cf_ref[0, :, pl.ds(pl.multiple_of(i * t, t), t)]
        cf_q_col = jnp.sum(jnp.where(row == col, jnp.broadcast_to(cf_q, (t, t)), 0.0),
                           axis=1, keepdims=True)
    else:
        allowed = (col // CHUNK) <= (row // CHUNK)

    def block(kj, m_run, l_run, acc, diagonal):
        start = pl.multiple_of(kj * t, t)
        kb = k_ref[0, pl.ds(start, t), :]
        vb = v_ref[0, pl.ds(start, t), :]
        s = _dot_nt(q, kb)
        if forget:
            s = s + cf_q_col - cf_ref[0, :, pl.ds(start, t)]
        if diagonal:
            s = jnp.where(allowed, s, NEG)
        m_new = jnp.maximum(m_run, jnp.max(s, axis=1, keepdims=True))
        alpha = jnp.exp2(m_run - m_new)
        p = jnp.exp2(s - m_new)
        l_run = alpha * l_run + jnp.sum(p, axis=1, keepdims=True)
        acc = alpha * acc + _dot(p.astype(BF16), vb)
        return m_new, l_run, acc

    init = (jnp.full((t, 1), NEG, F32), jnp.zeros((t, 1), F32), jnp.zeros((t, HEAD_DIM), F32))
    carry = block(i, *init, True)

    def body(n, carry):
        return block(i - 1 - n, *carry, False)

    _, l_run, acc = lax.fori_loop(0, i, body, carry)
    o_ref[0] = (acc / l_run * _silu(g_ref[0])).astype(BF16)


def _softmax_attention(q_arr, q_off, k_arr, k_off, v_arr, v_off, gate, batch, seq, cf=None):
    t = min(512, seq)
    nq = seq // t
    m = batch * seq
    h = N_HEADS
    qk_width = q_arr.shape[-1]
    forget = cf is not None
    in_specs = [
        pl.BlockSpec((1, t, qk_width), lambda hh, b, i: (q_off + hh, b * nq + i, 0)),
        pl.BlockSpec((1, seq, qk_width), lambda hh, b, i: (k_off + hh, b, 0)),
        pl.BlockSpec((1, seq, LANES), lambda hh, b, i: (v_off + hh, b, 0)),
        pl.BlockSpec((1, t, LANES), lambda hh, b, i: (hh, b * nq + i, 0)),
    ]
    args = [q_arr, k_arr, v_arr, gate]
    if forget:
        in_specs.append(pl.BlockSpec((1, 1, seq), lambda hh, b, i: (b * h + hh, 0, 0)))
        args.append(cf.reshape(batch * h, 1, seq))
    return pl.pallas_call(
        functools.partial(_softmax_kernel, t=t, forget=forget),
        grid=(h, batch, nq),
        in_specs=in_specs,
        out_specs=pl.BlockSpec((1, t, LANES), lambda hh, b, i: (hh, b * nq + i, 0)),
        out_shape=jax.ShapeDtypeStruct((h, m, LANES), BF16),
        compiler_params=_params("parallel", "parallel", "arbitrary"),
        name="forget_attention" if forget else "mla_attention",
    )(*args)


def _forget_cumsum_kernel(fl_ref, cf_ref, *, seq, width):
    row = lax.broadcasted_iota(jnp.int32, (width, width), 0)
    col = lax.broadcasted_iota(jnp.int32, (width, width), 1)
    prefix = (row <= col).astype(BF16)
    carry = jnp.zeros((N_HEADS, 1), F32)
    for c0 in range(0, seq, width):
        f = fl_ref[0, :, c0:c0 + width]
        log_f = jnp.minimum(f, 0.0) - jnp.log1p(jnp.exp(-jnp.abs(f)))
        cum = _split_dot(log_f, prefix, 3) + carry
        cf_ref[0, :, c0:c0 + width] = cum * LOG2E
        carry = cum[:, width - 1:width]


def _forget_cumsum(f_logit):
    batch, h, seq = f_logit.shape
    width = min(512, seq)
    return pl.pallas_call(
        functools.partial(_forget_cumsum_kernel, seq=seq, width=width),
        grid=(batch,),
        in_specs=[pl.BlockSpec((1, h, seq), lambda b: (b, 0, 0))],
        out_specs=pl.BlockSpec((1, h, seq), lambda b: (b, 0, 0)),
        out_shape=jax.ShapeDtypeStruct((batch, h, seq), F32),
        compiler_params=_params("parallel"),
        name="forget_cumsum",
    )(f_logit)


def _mla_proj_kernel(x_ref, pos_ref, inv_ref, g_ref, w1_ref, qn_ref, wq_ref, kvn_ref, wkv_ref,
                     qcat_ref, kcat_ref, v_ref, gate_ref):
    h = _rmsnorm(x_ref[...], g_ref[...]).astype(BF16)
    lat = _dot(h, w1_ref[:, :5 * LANES])
    q_lat = lat[:, :MLA_Q_RANK]
    kv_lat = lat[:, MLA_Q_RANK:MLA_Q_RANK + MLA_KV_RANK]
    k_a = lat[:, 3 * LANES:4 * LANES]
    k_b = lat[:, 4 * LANES:5 * LANES]

    lane = lax.broadcasted_iota(jnp.int32, (1, LANES), 1)
    half = MLA_ROPE_DIM // 2
    ang = pos_ref[...] * inv_ref[...]
    cos = jnp.cos(ang)
    sin = jnp.where((lane // half) % 2 == 0, -1.0, 1.0) * jnp.sin(ang)
    k_rope = (k_a * cos + k_b * sin).astype(BF16)

    kvn = _rmsnorm(kv_lat, kvn_ref[...]).astype(BF16)
    for c0 in range(0, 2 * N_HEADS, 8):
        kv = _dot(kvn, wkv_ref[:, c0 * LANES:(c0 + 8) * LANES])
        for c in range(8):
            piece = kv[:, c * LANES:(c + 1) * LANES].astype(BF16)
            if c0 + c < N_HEADS:
                kcat_ref[c0 + c] = jnp.concatenate([piece, k_rope], axis=1)
            else:
                v_ref[c0 + c - N_HEADS] = piece

    qn = _rmsnorm(q_lat, qn_ref[...]).astype(BF16)
    n_pairs = N_HEADS // 2
    rope_a = _dot(qn, wq_ref[:, N_HEADS * LANES:(N_HEADS + n_pairs) * LANES])
    rope_b = _dot(qn, wq_ref[:, (N_HEADS + n_pairs) * LANES:])
    first_head = lane < MLA_ROPE_DIM
    q_scale = (MLA_NOPE_DIM + MLA_ROPE_DIM) ** -0.5 * LOG2E
    for c0 in range(0, N_HEADS, 8):
        nope = _dot(qn, wq_ref[:, c0 * LANES:(c0 + 8) * LANES])
        for c in range(8):
            hh = c0 + c
            pair = hh // 2
            sl = slice(pair * LANES, (pair + 1) * LANES)
            q_rope = rope_a[:, sl] * cos + rope_b[:, sl] * sin
            mine = first_head if hh % 2 == 0 else jnp.logical_not(first_head)
            q_rope = jnp.where(mine, q_rope * q_scale, 0.0)
            q_nope = nope[:, c * LANES:(c + 1) * LANES] * q_scale
            qcat_ref[hh] = jnp.concatenate([q_nope.astype(BF16), q_rope.astype(BF16)], axis=1)

    for c0 in range(0, N_HEADS, 8):
        gt = _dot(h, w1_ref[:, (5 + c0) * LANES:(5 + c0 + 8) * LANES])
        for c in range(8):
            gate_ref[c0 + c] = gt[:, c * LANES:(c + 1) * LANES]


def _mla_weights(w_in, w_qb, w_kvb):
    i1 = MLA_Q_RANK
    i2 = i1 + MLA_KV_RANK
    i3 = i2 + MLA_ROPE_DIM
    half = MLA_ROPE_DIM // 2
    wk1, wk2 = w_in[:, i2:i2 + half], w_in[:, i2 + half:i3]
    w1 = jnp.concatenate([w_in[:, :i2], wk1, wk2, wk1, wk2, wk2, wk1, wk2, wk1, w_in[:, i3:]], axis=1)
    wq = w_qb.reshape(MLA_Q_RANK, N_HEADS, MLA_NOPE_DIM + MLA_ROPE_DIM)
    nope = wq[:, :, :MLA_NOPE_DIM].reshape(MLA_Q_RANK, -1)
    r1 = wq[:, :, MLA_NOPE_DIM:MLA_NOPE_DIM + half]
    r2 = wq[:, :, MLA_NOPE_DIM + half:]
    rope_a = jnp.concatenate([r1, r2], axis=2).reshape(MLA_Q_RANK, -1)
    rope_b = jnp.concatenate([r2, r1], axis=2).reshape(MLA_Q_RANK, -1)
    wq2 = jnp.concatenate([nope, rope_a, rope_b], axis=1)
    wkv = w_kvb.reshape(MLA_KV_RANK, N_HEADS, 2 * LANES)
    wkv2 = jnp.concatenate([wkv[:, :, :MLA_NOPE_DIM].reshape(MLA_KV_RANK, -1),
                            wkv[:, :, MLA_NOPE_DIM:].reshape(MLA_KV_RANK, -1)], axis=1)
    return w1.astype(BF16), wq2.astype(BF16), wkv2.astype(BF16)


def _mla_proj(x2d, positions, ln, w_in, q_norm, w_qb, kv_norm, w_kvb, tokens_per_batch):
    m, d = x2d.shape
    tm = min(256, tokens_per_batch)
    w1, wq2, wkv2 = _mla_weights(w_in, w_qb, w_kvb)
    half = MLA_ROPE_DIM // 2
    inv_freq = ROPE_BASE ** (-jnp.arange(0, MLA_ROPE_DIM, 2, dtype=F32) / MLA_ROPE_DIM)
    inv4 = jnp.tile(inv_freq, LANES // half).reshape(1, LANES)
    pos = positions.astype(F32).reshape(m, 1)
    h = N_HEADS
    head_major = lambda width, dtype: jax.ShapeDtypeStruct((h, m, width), dtype)
    head_block = lambda width: pl.BlockSpec((h, tm, width), lambda i: (0, i, 0))
    return pl.pallas_call(
        _mla_proj_kernel,
        grid=(m // tm,),
        in_specs=[pl.BlockSpec((tm, d), lambda i: (i, 0)),
                  pl.BlockSpec((tm, 1), lambda i: (i, 0)),
                  _resident((1, LANES)), _resident((1, d)), _resident(w1.shape),
                  _resident((1, MLA_Q_RANK)), _resident(wq2.shape),
                  _resident((1, MLA_KV_RANK)), _resident(wkv2.shape)],
        out_specs=[head_block(2 * LANES), head_block(2 * LANES), head_block(LANES), head_block(LANES)],
        out_shape=[head_major(2 * LANES, BF16), head_major(2 * LANES, BF16),
                   head_major(LANES, BF16), head_major(LANES, F32)],
        compiler_params=_params("parallel"),
        name="mla_proj",
    )(x2d, pos, inv4, ln.reshape(1, d), w1, q_norm.reshape(1, -1), wq2, kv_norm.reshape(1, -1), wkv2)


def kernel(x, positions, ln0, w_in0, w_out0, ln1, w_in1, q_norm1, w_qb1, kv_norm1, w_kvb1, w_out1,
           ln2, w_in2, b_f2, w_out2, ln3, w_in3, w_out3, final_norm):
    batch, seq, d = x.shape
    h = N_HEADS
    x2d = x.reshape(batch * seq, d)

    qkv, gate = _norm_proj(x2d, ln0, w_in0.astype(BF16), seq)
    o = _stick_attention(qkv, gate, batch, seq)
    x2d = _out_proj(o, w_out0.astype(BF16), x2d)

    qcat, kcat, v, gate = _mla_proj(x2d, positions, ln1, w_in1, q_norm1, w_qb1, kv_norm1, w_kvb1, seq)
    o = _softmax_attention(qcat, 0, kcat, 0, v, 0, gate, batch, seq)
    x2d = _out_proj(o, w_out1.astype(BF16), x2d)

    w2 = w_in2[:, :4 * D_INNER].astype(BF16)
    wf = w_in2[:, 4 * D_INNER:].T.astype(BF16)
    qkv, gate, f_logit = _norm_proj(x2d, ln2, w2, seq, wf=wf, bf=b_f2)
    cf = _forget_cumsum(f_logit)
    o = _softmax_attention(qkv, 0, qkv, h, qkv, 2 * h, gate, batch, seq, cf=cf)
    x2d = _out_proj(o, w_out2.astype(BF16), x2d)

    qkv, gate = _norm_proj(x2d, ln3, w_in3.astype(BF16), seq)
    o = _stick_attention(qkv, gate, batch, seq)
    x2d = _out_proj(o, w_out3.astype(BF16), x2d, final_g=final_norm)
    return x2d.reshape(batch, seq, d)
```

```python
import functools

import jax
import jax.numpy as jnp
from jax import lax
from jax.experimental import pallas as pl
from jax.experimental.pallas import tpu as pltpu

D_MODEL = 1024
N_HEADS = 16
HEAD_DIM = 128
D_INNER = N_HEADS * HEAD_DIM
CHUNK = 64
MLA_Q_RANK = 256
MLA_KV_RANK = 128
MLA_NOPE_DIM = 128
MLA_ROPE_DIM = 64
ROPE_BASE = 10000.0
EPS = 1e-6
NEG = -1e30
LOG2E = 1.4426950408889634
UNDERFLOW_LOG2 = 150.0

HEADS_PER_STEP = 2
KEY_SUB = 256
LANES = 128
VMEM_LIMIT_BYTES = 56 * 1024 * 1024

F32 = jnp.float32
BF16 = jnp.bfloat16


def _params(*semantics):
    return pltpu.CompilerParams(dimension_semantics=semantics,
                                vmem_limit_bytes=VMEM_LIMIT_BYTES)


def _resident(shape):
    zeros = (0,) * len(shape)
    return pl.BlockSpec(shape, lambda *_: zeros, pipeline_mode=pl.Buffered(1))


def _rmsnorm(x, g):
    return x * lax.rsqrt(jnp.mean(x * x, axis=-1, keepdims=True) + EPS) * g


def _dot(a, b):
    return jnp.dot(a, b, preferred_element_type=F32)


def _dot_nt(a, b):
    return lax.dot_general(a, b, (((1,), (1,)), ((), ())), preferred_element_type=F32)


def _bf16_terms(a, terms):
    out = []
    for t in range(terms):
        part = a.astype(BF16)
        out.append(part)
        if t + 1 < terms:
            a = a - part.astype(F32)
    return out


def _silu(g):
    return g * (1.0 / (1.0 + jnp.exp(-g)))


def _head_major(m, width, dtype):
    return jax.ShapeDtypeStruct((N_HEADS, m, width), dtype)


def _head_block(tm, width):
    return pl.BlockSpec((N_HEADS, tm, width), lambda i: (0, i, 0))


def _proj_kernel(x_ref, g_ref, w_ref, qkv_ref, gate_ref):
    h = _rmsnorm(x_ref[...], g_ref[...]).astype(BF16)
    n_bf16 = 3 * N_HEADS
    step = 8
    for c0 in range(0, 4 * N_HEADS, step):
        acc = _dot(h, w_ref[:, c0 * LANES:(c0 + step) * LANES])
        for c in range(step):
            piece = acc[:, c * LANES:(c + 1) * LANES]
            if c0 + c < N_HEADS:
                qkv_ref[c0 + c] = (piece * (HEAD_DIM ** -0.5 * LOG2E)).astype(BF16)
            elif c0 + c < n_bf16:
                qkv_ref[c0 + c] = piece.astype(BF16)
            else:
                gate_ref[c0 + c - n_bf16] = piece


def _norm_proj(x2d, g, w, tokens_per_batch):
    m, d = x2d.shape
    tm = min(256, tokens_per_batch)
    n_bf16 = 3 * N_HEADS
    return pl.pallas_call(
        _proj_kernel,
        grid=(m // tm,),
        in_specs=[pl.BlockSpec((tm, d), lambda i: (i, 0)), _resident((1, d)), _resident(w.shape)],
        out_specs=[pl.BlockSpec((n_bf16, tm, LANES), lambda i: (0, i, 0)), _head_block(tm, LANES)],
        out_shape=[jax.ShapeDtypeStruct((n_bf16, m, LANES), BF16), _head_major(m, LANES, F32)],
        compiler_params=_params("parallel"),
        name="norm_proj",
    )(x2d, g.reshape(1, d), w)


def _forget_proj_kernel(x_ref, g_ref, w_ref, wvt_ref, wf_ref, bf_ref,
                        qcat_ref, kcat_ref, vt_ref, gate_ref, carry_ref, *, per_batch):
    i = pl.program_id(0)
    tm = x_ref.shape[0]
    h = _rmsnorm(x_ref[...], g_ref[...]).astype(BF16)

    @pl.when(i % per_batch == 0)
    def _():
        carry_ref[...] = jnp.zeros_like(carry_ref)

    f = _dot(h, wf_ref[...]) + bf_ref[...]
    log_f = (jnp.minimum(f, 0.0) - jnp.log1p(jnp.exp(-jnp.abs(f)))) * LOG2E
    row = lax.broadcasted_iota(jnp.int32, (tm, tm), 0)
    col = lax.broadcasted_iota(jnp.int32, (tm, tm), 1)
    up_to = (col <= row).astype(BF16)
    cf = carry_ref[0:1, :]
    for part in _bf16_terms(log_f, 3):
        cf = cf + _dot(up_to, part)
    carry_ref[...] = jnp.broadcast_to(cf[tm - 1:tm, :], carry_ref.shape)

    lane = lax.broadcasted_iota(jnp.int32, (1, LANES), 1)
    for hh in range(N_HEADS):
        hi, mid, lo = (t.astype(F32) for t in _bf16_terms(jnp.broadcast_to(cf[:, hh:hh + 1], (tm, LANES)), 3))
        terms = jnp.where(lane % 3 == 0, hi, jnp.where(lane % 3 == 1, mid, lo))
        q_ext = jnp.where(lane < 3, terms, jnp.where(lane < 6, 1.0, 0.0))
        k_ext = jnp.where(lane < 3, 1.0, jnp.where(lane < 6, -terms, 0.0))
        qcat_ref[hh, :, LANES:] = q_ext.astype(BF16)
        kcat_ref[hh, :, LANES:] = k_ext.astype(BF16)

    step = 8
    for c0 in range(0, 3 * N_HEADS, step):
        acc = _dot(h, w_ref[:, c0 * LANES:(c0 + step) * LANES])
        for c in range(step):
            piece = acc[:, c * LANES:(c + 1) * LANES]
            g_idx = c0 + c
            if g_idx < N_HEADS:
                qcat_ref[g_idx, :, :LANES] = (piece * (HEAD_DIM ** -0.5 * LOG2E)).astype(BF16)
            elif g_idx < 2 * N_HEADS:
                kcat_ref[g_idx - N_HEADS, :, :LANES] = piece.astype(BF16)
            else:
                gate_ref[g_idx - 2 * N_HEADS] = piece
    for c0 in range(0, N_HEADS, step):
        vt = _dot_nt(wvt_ref[c0 * LANES:(c0 + step) * LANES, :], h)
        for c in range(step):
            vt_ref[c0 + c] = vt[c * LANES:(c + 1) * LANES, :].astype(BF16)


def _forget_proj(x2d, g, w_in, b_f, tokens_per_batch):
    m, d = x2d.shape
    tm = min(256, tokens_per_batch)
    di = D_INNER
    w = jnp.concatenate([w_in[:, :2 * di], w_in[:, 3 * di:4 * di]], axis=1).astype(BF16)
    wvt = w_in[:, 2 * di:3 * di].T.astype(BF16)
    wf = jnp.pad(w_in[:, 4 * di:], ((0, 0), (0, LANES - N_HEADS))).astype(BF16)
    bf = jnp.pad(b_f, (0, LANES - N_HEADS)).reshape(1, LANES)
    return pl.pallas_call(
        functools.partial(_forget_proj_kernel, per_batch=tokens_per_batch // tm),
        grid=(m // tm,),
        in_specs=[pl.BlockSpec((tm, d), lambda i: (i, 0)), _resident((1, d)), _resident(w.shape),
                  _resident(wvt.shape), _resident(wf.shape), _resident(bf.shape)],
        out_specs=[_head_block(tm, 2 * LANES), _head_block(tm, 2 * LANES),
                   pl.BlockSpec((N_HEADS, LANES, tm), lambda i: (0, 0, i)), _head_block(tm, LANES)],
        out_shape=[_head_major(m, 2 * LANES, BF16), _head_major(m, 2 * LANES, BF16),
                   jax.ShapeDtypeStruct((N_HEADS, LANES, m), BF16), _head_major(m, LANES, F32)],
        scratch_shapes=[pltpu.VMEM((8, LANES), F32)],
        compiler_params=_params("arbitrary"),
        name="forget_proj",
    )(x2d, g.reshape(1, d), w, wvt, wf, bf)


def _mla_proj_kernel(x_ref, pos_ref, inv_ref, g_ref, w1_ref, qn_ref, wq_ref, kvn_ref, wk_ref, wvt_ref,
                     qcat_ref, kcat_ref, vt_ref, gate_ref):
    h = _rmsnorm(x_ref[...], g_ref[...]).astype(BF16)
    lat = _dot(h, w1_ref[:, :5 * LANES])
    q_lat = lat[:, :MLA_Q_RANK]
    kv_lat = lat[:, MLA_Q_RANK:MLA_Q_RANK + MLA_KV_RANK]
    k_a = lat[:, 3 * LANES:4 * LANES]
    k_b = lat[:, 4 * LANES:5 * LANES]

    lane = lax.broadcasted_iota(jnp.int32, (1, LANES), 1)
    half = MLA_ROPE_DIM // 2
    ang = pos_ref[...] * inv_ref[...]
    cos = jnp.cos(ang)
    sin = jnp.where((lane // half) % 2 == 0, -1.0, 1.0) * jnp.sin(ang)
    k_rope = (k_a * cos + k_b * sin).astype(BF16)

    kvn = _rmsnorm(kv_lat, kvn_ref[...]).astype(BF16)
    step = 8
    for c0 in range(0, N_HEADS, step):
        k_nope = _dot(kvn, wk_ref[:, c0 * LANES:(c0 + step) * LANES])
        vt = _dot_nt(wvt_ref[c0 * LANES:(c0 + step) * LANES, :], kvn)
        for c in range(step):
            kcat_ref[c0 + c] = jnp.concatenate(
                [k_nope[:, c * LANES:(c + 1) * LANES].astype(BF16), k_rope], axis=1)
            vt_ref[c0 + c] = vt[c * LANES:(c + 1) * LANES, :].astype(BF16)

    qn = _rmsnorm(q_lat, qn_ref[...]).astype(BF16)
    n_pairs = N_HEADS // 2
    rope_a = _dot(qn, wq_ref[:, N_HEADS * LANES:(N_HEADS + n_pairs) * LANES])
    rope_b = _dot(qn, wq_ref[:, (N_HEADS + n_pairs) * LANES:])
    first_head = lane < MLA_ROPE_DIM
    q_scale = (MLA_NOPE_DIM + MLA_ROPE_DIM) ** -0.5 * LOG2E
    for c0 in range(0, N_HEADS, step):
        nope = _dot(qn, wq_ref[:, c0 * LANES:(c0 + step) * LANES])
        for c in range(step):
            hh = c0 + c
            pair = hh // 2
            sl = slice(pair * LANES, (pair + 1) * LANES)
            q_rope = rope_a[:, sl] * cos + rope_b[:, sl] * sin
            mine = first_head if hh % 2 == 0 else jnp.logical_not(first_head)
            q_rope = jnp.where(mine, q_rope * q_scale, 0.0)
            q_nope = nope[:, c * LANES:(c + 1) * LANES] * q_scale
            qcat_ref[hh] = jnp.concatenate([q_nope.astype(BF16), q_rope.astype(BF16)], axis=1)

    for c0 in range(0, N_HEADS, step):
        gt = _dot(h, w1_ref[:, (5 + c0) * LANES:(5 + c0 + step) * LANES])
        for c in range(step):
            gate_ref[c0 + c] = gt[:, c * LANES:(c + 1) * LANES]


def _mla_weights(w_in, w_qb, w_kvb):
    i1 = MLA_Q_RANK
    i2 = i1 + MLA_KV_RANK
    i3 = i2 + MLA_ROPE_DIM
    half = MLA_ROPE_DIM // 2
    wk1, wk2 = w_in[:, i2:i2 + half], w_in[:, i2 + half:i3]
    w1 = jnp.concatenate([w_in[:, :i2], wk1, wk2, wk1, wk2, wk2, wk1, wk2, wk1, w_in[:, i3:]], axis=1)
    wq = w_qb.reshape(MLA_Q_RANK, N_HEADS, MLA_NOPE_DIM + MLA_ROPE_DIM)
    nope = wq[:, :, :MLA_NOPE_DIM].reshape(MLA_Q_RANK, -1)
    r1 = wq[:, :, MLA_NOPE_DIM:MLA_NOPE_DIM + half]
    r2 = wq[:, :, MLA_NOPE_DIM + half:]
    rope_a = jnp.concatenate([r1, r2], axis=2).reshape(MLA_Q_RANK, -1)
    rope_b = jnp.concatenate([r2, r1], axis=2).reshape(MLA_Q_RANK, -1)
    wq2 = jnp.concatenate([nope, rope_a, rope_b], axis=1)
    wkv = w_kvb.reshape(MLA_KV_RANK, N_HEADS, 2 * LANES)
    wk = wkv[:, :, :MLA_NOPE_DIM].reshape(MLA_KV_RANK, -1)
    wvt = wkv[:, :, MLA_NOPE_DIM:].reshape(MLA_KV_RANK, -1).T
    return w1.astype(BF16), wq2.astype(BF16), wk.astype(BF16), wvt.astype(BF16)


def _mla_proj(x2d, positions, ln, w_in, q_norm, w_qb, kv_norm, w_kvb, tokens_per_batch):
    m, d = x2d.shape
    tm = min(256, tokens_per_batch)
    w1, wq2, wk, wvt = _mla_weights(w_in, w_qb, w_kvb)
    half = MLA_ROPE_DIM // 2
    inv_freq = ROPE_BASE ** (-jnp.arange(0, MLA_ROPE_DIM, 2, dtype=F32) / MLA_ROPE_DIM)
    inv4 = jnp.tile(inv_freq, LANES // half).reshape(1, LANES)
    pos = positions.astype(F32).reshape(m, 1)
    return pl.pallas_call(
        _mla_proj_kernel,
        grid=(m // tm,),
        in_specs=[pl.BlockSpec((tm, d), lambda i: (i, 0)),
                  pl.BlockSpec((tm, 1), lambda i: (i, 0)),
                  _resident((1, LANES)), _resident((1, d)), _resident(w1.shape),
                  _resident((1, MLA_Q_RANK)), _resident(wq2.shape),
                  _resident((1, MLA_KV_RANK)), _resident(wk.shape), _resident(wvt.shape)],
        out_specs=[_head_block(tm, 2 * LANES), _head_block(tm, 2 * LANES),
                   pl.BlockSpec((N_HEADS, LANES, tm), lambda i: (0, 0, i)), _head_block(tm, LANES)],
        out_shape=[_head_major(m, 2 * LANES, BF16), _head_major(m, 2 * LANES, BF16),
                   jax.ShapeDtypeStruct((N_HEADS, LANES, m), BF16), _head_major(m, LANES, F32)],
        compiler_params=_params("parallel"),
        name="mla_proj",
    )(x2d, pos, inv4, ln.reshape(1, d), w1, q_norm.reshape(1, -1), wq2, kv_norm.reshape(1, -1), wk, wvt)


def _out_kernel(o_ref, w_ref, x_ref, *rest, final):
    if final:
        g_ref, y_ref = rest
    else:
        (y_ref,) = rest
    o = jnp.concatenate([o_ref[hh] for hh in range(N_HEADS)], axis=1)
    y = x_ref[...] + _dot(o, w_ref[...])
    if final:
        y = _rmsnorm(y, g_ref[...])
    y_ref[...] = y


def _out_proj(o, w_out, x2d, final_g=None):
    m, d = x2d.shape
    tm = min(512, m)
    final = final_g is not None
    in_specs = [_head_block(tm, LANES), _resident((D_INNER, d)), pl.BlockSpec((tm, d), lambda i: (i, 0))]
    args = [o, w_out, x2d]
    if final:
        in_specs.append(_resident((1, d)))
        args.append(final_g.reshape(1, d))
    return pl.pallas_call(
        functools.partial(_out_kernel, final=final),
        grid=(m // tm,),
        in_specs=in_specs,
        out_specs=pl.BlockSpec((tm, d), lambda i: (i, 0)),
        out_shape=jax.ShapeDtypeStruct((m, d), F32),
        compiler_params=_params("parallel"),
        name="out_proj_final" if final else "out_proj",
    )(*args)


def _stick_kernel(q_ref, k_ref, v_ref, g_ref, suffix_ref, o_ref, *, t):
    i = pl.program_id(2)
    n_heads = q_ref.shape[0]
    row = lax.broadcasted_iota(jnp.int32, (t, t), 0)
    col = lax.broadcasted_iota(jnp.int32, (t, t), 1)
    strict = col < row

    def keys(kj):
        return pl.ds(pl.multiple_of(kj * t, t), t)

    def skips(hh, kj):
        z = _dot_nt(q_ref[hh], k_ref[hh, keys(kj), :])
        u = jnp.maximum(z, 0.0) + jnp.log2(1.0 + jnp.exp2(-jnp.abs(z)))
        return z, u

    def weights(z, u, c):
        later = _dot(jnp.concatenate(_bf16_terms(u, 2), axis=1), suffix_ref[...])
        return jnp.exp2(z - later + c)

    def row_sum(u):
        return jnp.sum(u, axis=1, keepdims=True)

    def first_two(hh):
        z, u = skips(hh, i)
        u = jnp.where(strict, u, 0.0)
        w = jnp.where(strict, weights(z, u, 0.0), 0.0)
        acc = _dot(w.astype(BF16), v_ref[hh, keys(i), :])
        c = -row_sum(u)
        prev = jnp.maximum(i - 1, 0)
        z, u = skips(hh, prev)
        w = weights(z, u, c)
        vb = v_ref[hh, keys(prev), :]
        acc = acc + _dot(w.astype(BF16), jnp.where(i > 0, vb, jnp.zeros_like(vb)))
        return c - row_sum(u), acc

    state = tuple(first_two(hh) for hh in range(n_heads))

    def more(carry):
        kj, state = carry
        c_max = functools.reduce(jnp.maximum, [jnp.max(c) for c, _ in state])
        return jnp.logical_and(kj >= 0, c_max > -UNDERFLOW_LOG2)

    def body(carry):
        kj, state = carry
        new_state = []
        for hh, (c, acc) in enumerate(state):
            z, u = skips(hh, kj)
            w = weights(z, u, c)
            acc = acc + _dot(w.astype(BF16), v_ref[hh, keys(kj), :])
            new_state.append((c - row_sum(u), acc))
        return kj - 1, tuple(new_state)

    _, state = lax.while_loop(more, body, (i - 2, state))
    for hh, (_, acc) in enumerate(state):
        o_ref[hh] = (acc * _silu(g_ref[hh])).astype(BF16)


def _stick_attention(qkv, gate, batch, seq):
    t = min(256, seq)
    nq = seq // t
    m = batch * seq
    h = N_HEADS
    hp = HEADS_PER_STEP
    groups = h // hp
    from_here = jnp.tril(jnp.ones((t, t), BF16))
    suffix_sum = jnp.concatenate([from_here, from_here], axis=0)
    return pl.pallas_call(
        functools.partial(_stick_kernel, t=t),
        grid=(groups, batch, nq),
        in_specs=[
            pl.BlockSpec((hp, t, LANES), lambda hh, b, i: (hh, b * nq + i, 0)),
            pl.BlockSpec((hp, seq, LANES), lambda hh, b, i: (groups + hh, b, 0)),
            pl.BlockSpec((hp, seq, LANES), lambda hh, b, i: (2 * groups + hh, b, 0)),
            pl.BlockSpec((hp, t, LANES), lambda hh, b, i: (hh, b * nq + i, 0)),
            _resident(suffix_sum.shape),
        ],
        out_specs=pl.BlockSpec((hp, t, LANES), lambda hh, b, i: (hh, b * nq + i, 0)),
        out_shape=jax.ShapeDtypeStruct((h, m, LANES), BF16),
        compiler_params=_params("parallel", "parallel", "arbitrary"),
        name="stick_attention",
    )(qkv, qkv, qkv, gate, suffix_sum)


def _softmax_kernel(q_ref, k_ref, vt_ref, g_ref, o_ref, s_ref, p_ref, *, t, chunked):
    i = pl.program_id(2)
    n_heads = q_ref.shape[0]
    key = lax.broadcasted_iota(jnp.int32, (t, t), 0)
    qry = lax.broadcasted_iota(jnp.int32, (t, t), 1)
    allowed = (key // CHUNK) <= (qry // CHUNK) if chunked else key <= qry

    subs = t // KEY_SUB
    top = (i + 1) * subs - 1

    def key_slice(g):
        return pl.ds(pl.multiple_of(g * KEY_SUB, KEY_SUB), KEY_SUB)

    def scores(hh, g, slot):
        s_ref[hh, slot] = _dot_nt(k_ref[hh, key_slice(jnp.maximum(g, 0)), :], q_ref[hh])

    def weighted_values(hh, g, slot):
        return _dot(vt_ref[hh, :, key_slice(g)], p_ref[hh, slot])

    def step(hh, g, slot, mask, m_run, l_run, alpha_prev, acc):
        scores(hh, g - 1, 1 - slot)
        pv_prev = weighted_values(hh, jnp.minimum(g + 1, top), 1 - slot)
        if mask is not None:
            s_ref[hh, slot] = jnp.where(mask, s_ref[hh, slot], NEG)
        m_new = jnp.maximum(m_run, jnp.max(s_ref[hh, slot], axis=0, keepdims=True))
        alpha = jnp.exp2(m_run - m_new)
        p = jnp.exp2(s_ref[hh, slot] - m_new)
        p_ref[hh, slot] = p.astype(BF16)
        l_run = alpha * l_run + jnp.sum(p, axis=0, keepdims=True)
        return m_new, l_run, alpha, alpha_prev * acc + pv_prev

    def tile(hh, first, state, diagonal):
        for j in range(subs):
            mask = None
            if diagonal:
                local = (subs - 1 - j) * KEY_SUB
                mask = allowed[local:local + KEY_SUB]
            state = step(hh, first - j, j % 2, mask, *state)
        return state

    init = (jnp.full((1, t), NEG, F32), jnp.zeros((1, t), F32), jnp.ones((1, t), F32),
            jnp.zeros((HEAD_DIM, t), F32))
    for hh in range(n_heads):
        scores(hh, top, 0)
        p_ref[hh, 1] = jnp.zeros(p_ref.shape[2:], BF16)
    carry = tuple(tile(hh, top, init, True) for hh in range(n_heads))

    def body(n, carry):
        return tuple(tile(hh, top - subs * (n + 1), carry[hh], False) for hh in range(n_heads))

    carry = lax.fori_loop(0, i, body, carry)
    for hh in range(n_heads):
        _, l_run, alpha, acc = carry[hh]
        acc = alpha * acc + weighted_values(hh, 0, 1)
        o_ref[hh] = ((acc / l_run).T * _silu(g_ref[hh])).astype(BF16)


def _softmax_attention(qcat, kcat, vt, gate, batch, seq, chunked):
    t = min(1024, seq)
    nq = seq // t
    m = batch * seq
    h = N_HEADS
    width = qcat.shape[-1]
    hp = HEADS_PER_STEP
    return pl.pallas_call(
        functools.partial(_softmax_kernel, t=t, chunked=chunked),
        grid=(h // hp, batch, nq),
        in_specs=[
            pl.BlockSpec((hp, t, width), lambda hh, b, i: (hh, b * nq + i, 0)),
            pl.BlockSpec((hp, seq, width), lambda hh, b, i: (hh, b, 0)),
            pl.BlockSpec((hp, LANES, seq), lambda hh, b, i: (hh, 0, b)),
            pl.BlockSpec((hp, t, LANES), lambda hh, b, i: (hh, b * nq + i, 0)),
        ],
        out_specs=pl.BlockSpec((hp, t, LANES), lambda hh, b, i: (hh, b * nq + i, 0)),
        scratch_shapes=[pltpu.VMEM((hp, 2, min(KEY_SUB, t), t), F32),
                        pltpu.VMEM((hp, 2, min(KEY_SUB, t), t), BF16)],
        out_shape=jax.ShapeDtypeStruct((h, m, LANES), BF16),
        compiler_params=_params("parallel", "parallel", "arbitrary"),
        name="mla_attention" if chunked else "forget_attention",
    )(qcat, kcat, vt, gate)


def kernel(x, positions, ln0, w_in0, w_out0, ln1, w_in1, q_norm1, w_qb1, kv_norm1, w_kvb1, w_out1,
           ln2, w_in2, b_f2, w_out2, ln3, w_in3, w_out3, final_norm):
    batch, seq, d = x.shape
    x2d = x.reshape(batch * seq, d)

    qkv, gate = _norm_proj(x2d, ln0, w_in0.astype(BF16), seq)
    o = _stick_attention(qkv, gate, batch, seq)
    x2d = _out_proj(o, w_out0.astype(BF16), x2d)

    qcat, kcat, vt, gate = _mla_proj(x2d, positions, ln1, w_in1, q_norm1, w_qb1, kv_norm1, w_kvb1, seq)
    o = _softmax_attention(qcat, kcat, vt, gate, batch, seq, chunked=True)
    x2d = _out_proj(o, w_out1.astype(BF16), x2d)

    qcat, kcat, vt, gate = _forget_proj(x2d, ln2, w_in2, b_f2, seq)
    o = _softmax_attention(qcat, kcat, vt, gate, batch, seq, chunked=False)
    x2d = _out_proj(o, w_out2.astype(BF16), x2d)

    qkv, gate = _norm_proj(x2d, ln3, w_in3.astype(BF16), seq)
    o = _stick_attention(qkv, gate, batch, seq)
    x2d = _out_proj(o, w_out3.astype(BF16), x2d, final_g=final_norm)
    return x2d.reshape(batch, seq, d)
```

```python
import functools

import jax
import jax.numpy as jnp
from jax import lax
from jax.experimental import pallas as pl
from jax.experimental.pallas import tpu as pltpu

D_MODEL = 1024
N_HEADS = 16
HEAD_DIM = 128
D_INNER = N_HEADS * HEAD_DIM
CHUNK = 64
MLA_Q_RANK = 256
MLA_KV_RANK = 128
MLA_NOPE_DIM = 128
MLA_ROPE_DIM = 64
ROPE_BASE = 10000.0
EPS = 1e-6
NEG = -1e30
LOG2E = 1.4426950408889634
UNDERFLOW_LOG2 = 150.0

HEADS_PER_STEP = 2
STICK_HEADS_PER_STEP = 4
KEY_SUB = 512
LANES = 128
VMEM_LIMIT_BYTES = 56 * 1024 * 1024

F32 = jnp.float32
BF16 = jnp.bfloat16


def _params(*semantics):
    return pltpu.CompilerParams(dimension_semantics=semantics,
                                vmem_limit_bytes=VMEM_LIMIT_BYTES)


def _resident(shape):
    zeros = (0,) * len(shape)
    return pl.BlockSpec(shape, lambda *_: zeros, pipeline_mode=pl.Buffered(1))


def _rmsnorm(x, g):
    return x * lax.rsqrt(jnp.mean(x * x, axis=-1, keepdims=True) + EPS) * g


def _dot(a, b):
    return jnp.dot(a, b, preferred_element_type=F32)


def _dot_nt(a, b):
    return lax.dot_general(a, b, (((1,), (1,)), ((), ())), preferred_element_type=F32)


def _bf16_terms(a, terms):
    out = []
    for t in range(terms):
        part = a.astype(BF16)
        out.append(part)
        if t + 1 < terms:
            a = a - part.astype(F32)
    return out


def _silu(g):
    return g * (1.0 / (1.0 + jnp.exp(-g)))


def _head_major(m, width, dtype):
    return jax.ShapeDtypeStruct((N_HEADS, m, width), dtype)


def _head_block(tm, width):
    return pl.BlockSpec((N_HEADS, tm, width), lambda i: (0, i, 0))


def _proj_kernel(x_ref, g_ref, w_ref, qkv_ref, gate_ref):
    h = _rmsnorm(x_ref[...], g_ref[...]).astype(BF16)
    n_bf16 = 3 * N_HEADS
    step = 8
    for c0 in range(0, 4 * N_HEADS, step):
        acc = _dot(h, w_ref[:, c0 * LANES:(c0 + step) * LANES])
        for c in range(step):
            piece = acc[:, c * LANES:(c + 1) * LANES]
            if c0 + c < N_HEADS:
                qkv_ref[c0 + c] = (piece * (HEAD_DIM ** -0.5 * LOG2E)).astype(BF16)
            elif c0 + c < n_bf16:
                qkv_ref[c0 + c] = piece.astype(BF16)
            else:
                gate_ref[c0 + c - n_bf16] = piece


def _norm_proj(x2d, g, w, tokens_per_batch):
    m, d = x2d.shape
    tm = min(256, tokens_per_batch)
    n_bf16 = 3 * N_HEADS
    return pl.pallas_call(
        _proj_kernel,
        grid=(m // tm,),
        in_specs=[pl.BlockSpec((tm, d), lambda i: (i, 0)), _resident((1, d)), _resident(w.shape)],
        out_specs=[pl.BlockSpec((n_bf16, tm, LANES), lambda i: (0, i, 0)), _head_block(tm, LANES)],
        out_shape=[jax.ShapeDtypeStruct((n_bf16, m, LANES), BF16), _head_major(m, LANES, F32)],
        compiler_params=_params("parallel"),
        name="norm_proj",
    )(x2d, g.reshape(1, d), w)


def _forget_proj_kernel(x_ref, g_ref, w_ref, wvt_ref, wf_ref, bf_ref,
                        qcat_ref, kcat_ref, vt_ref, gate_ref, carry_ref, *, per_batch):
    i = pl.program_id(0)
    tm = x_ref.shape[0]
    h = _rmsnorm(x_ref[...], g_ref[...]).astype(BF16)

    @pl.when(i % per_batch == 0)
    def _():
        carry_ref[...] = jnp.zeros_like(carry_ref)

    f = _dot(h, wf_ref[...]) + bf_ref[...]
    log_f = (jnp.minimum(f, 0.0) - jnp.log1p(jnp.exp(-jnp.abs(f)))) * LOG2E
    row = lax.broadcasted_iota(jnp.int32, (tm, tm), 0)
    col = lax.broadcasted_iota(jnp.int32, (tm, tm), 1)
    up_to = (col <= row).astype(BF16)
    cf = carry_ref[0:1, :]
    for part in _bf16_terms(log_f, 3):
        cf = cf + _dot(up_to, part)
    carry_ref[...] = jnp.broadcast_to(cf[tm - 1:tm, :], carry_ref.shape)

    hi, mid, lo = (part.astype(F32) for part in _bf16_terms(cf, 3))
    lane = lax.broadcasted_iota(jnp.int32, (1, LANES), 1)
    for hh in range(N_HEADS):
        terms = jnp.where(lane % 3 == 0, hi[:, hh:hh + 1],
                          jnp.where(lane % 3 == 1, mid[:, hh:hh + 1], lo[:, hh:hh + 1]))
        q_ext = jnp.where(lane < 3, terms, jnp.where(lane < 6, 1.0, 0.0))
        k_ext = jnp.where(lane < 3, 1.0, jnp.where(lane < 6, -terms, 0.0))
        qcat_ref[hh, :, LANES:] = q_ext.astype(BF16)
        kcat_ref[hh, :, LANES:] = k_ext.astype(BF16)

    step = 8
    for c0 in range(0, 3 * N_HEADS, step):
        acc = _dot(h, w_ref[:, c0 * LANES:(c0 + step) * LANES])
        for c in range(step):
            piece = acc[:, c * LANES:(c + 1) * LANES]
            g_idx = c0 + c
            if g_idx < N_HEADS:
                qcat_ref[g_idx, :, :LANES] = (piece * (HEAD_DIM ** -0.5 * LOG2E)).astype(BF16)
            elif g_idx < 2 * N_HEADS:
                kcat_ref[g_idx - N_HEADS, :, :LANES] = piece.astype(BF16)
            else:
                gate_ref[g_idx - 2 * N_HEADS] = piece
    for c0 in range(0, N_HEADS, step):
        vt = _dot_nt(wvt_ref[c0 * LANES:(c0 + step) * LANES, :], h)
        for c in range(step):
            vt_ref[c0 + c] = vt[c * LANES:(c + 1) * LANES, :].astype(BF16)


def _forget_proj(x2d, g, w_in, b_f, tokens_per_batch):
    m, d = x2d.shape
    tm = min(256, tokens_per_batch)
    di = D_INNER
    w = jnp.concatenate([w_in[:, :2 * di], w_in[:, 3 * di:4 * di]], axis=1).astype(BF16)
    wvt = w_in[:, 2 * di:3 * di].T.astype(BF16)
    wf = jnp.pad(w_in[:, 4 * di:], ((0, 0), (0, LANES - N_HEADS))).astype(BF16)
    bf = jnp.pad(b_f, (0, LANES - N_HEADS)).reshape(1, LANES)
    return pl.pallas_call(
        functools.partial(_forget_proj_kernel, per_batch=tokens_per_batch // tm),
        grid=(m // tm,),
        in_specs=[pl.BlockSpec((tm, d), lambda i: (i, 0)), _resident((1, d)), _resident(w.shape),
                  _resident(wvt.shape), _resident(wf.shape), _resident(bf.shape)],
        out_specs=[_head_block(tm, 2 * LANES), _head_block(tm, 2 * LANES),
                   pl.BlockSpec((N_HEADS, LANES, tm), lambda i: (0, 0, i)), _head_block(tm, LANES)],
        out_shape=[_head_major(m, 2 * LANES, BF16), _head_major(m, 2 * LANES, BF16),
                   jax.ShapeDtypeStruct((N_HEADS, LANES, m), BF16), _head_major(m, LANES, F32)],
        scratch_shapes=[pltpu.VMEM((8, LANES), F32)],
        compiler_params=_params("arbitrary"),
        name="forget_proj",
    )(x2d, g.reshape(1, d), w, wvt, wf, bf)


def _mla_proj_kernel(x_ref, pos_ref, inv_ref, g_ref, w1_ref, qn_ref, wq_ref, kvn_ref, wk_ref, wvt_ref,
                     qcat_ref, kcat_ref, vt_ref, gate_ref):
    h = _rmsnorm(x_ref[...], g_ref[...]).astype(BF16)
    lat = _dot(h, w1_ref[:, :5 * LANES])
    q_lat = lat[:, :MLA_Q_RANK]
    kv_lat = lat[:, MLA_Q_RANK:MLA_Q_RANK + MLA_KV_RANK]
    k_a = lat[:, 3 * LANES:4 * LANES]
    k_b = lat[:, 4 * LANES:5 * LANES]

    lane = lax.broadcasted_iota(jnp.int32, (1, LANES), 1)
    half = MLA_ROPE_DIM // 2
    ang = pos_ref[...] * inv_ref[...]
    cos = jnp.cos(ang)
    sin = jnp.where((lane // half) % 2 == 0, -1.0, 1.0) * jnp.sin(ang)
    k_rope = (k_a * cos + k_b * sin).astype(BF16)

    kvn = _rmsnorm(kv_lat, kvn_ref[...]).astype(BF16)
    step = 8
    for c0 in range(0, N_HEADS, step):
        k_nope = _dot(kvn, wk_ref[:, c0 * LANES:(c0 + step) * LANES])
        vt = _dot_nt(wvt_ref[c0 * LANES:(c0 + step) * LANES, :], kvn)
        for c in range(step):
            kcat_ref[c0 + c] = jnp.concatenate(
                [k_nope[:, c * LANES:(c + 1) * LANES].astype(BF16), k_rope], axis=1)
            vt_ref[c0 + c] = vt[c * LANES:(c + 1) * LANES, :].astype(BF16)

    qn = _rmsnorm(q_lat, qn_ref[...]).astype(BF16)
    n_pairs = N_HEADS // 2
    rope_a = _dot(qn, wq_ref[:, N_HEADS * LANES:(N_HEADS + n_pairs) * LANES])
    rope_b = _dot(qn, wq_ref[:, (N_HEADS + n_pairs) * LANES:])
    first_head = lane < MLA_ROPE_DIM
    q_scale = (MLA_NOPE_DIM + MLA_ROPE_DIM) ** -0.5 * LOG2E
    for c0 in range(0, N_HEADS, step):
        nope = _dot(qn, wq_ref[:, c0 * LANES:(c0 + step) * LANES])
        for c in range(step):
            hh = c0 + c
            pair = hh // 2
            sl = slice(pair * LANES, (pair + 1) * LANES)
            q_rope = rope_a[:, sl] * cos + rope_b[:, sl] * sin
            mine = first_head if hh % 2 == 0 else jnp.logical_not(first_head)
            q_rope = jnp.where(mine, q_rope * q_scale, 0.0)
            q_nope = nope[:, c * LANES:(c + 1) * LANES] * q_scale
            qcat_ref[hh] = jnp.concatenate([q_nope.astype(BF16), q_rope.astype(BF16)], axis=1)

    for c0 in range(0, N_HEADS, step):
        gt = _dot(h, w1_ref[:, (5 + c0) * LANES:(5 + c0 + step) * LANES])
        for c in range(step):
            gate_ref[c0 + c] = gt[:, c * LANES:(c + 1) * LANES]


def _mla_weights(w_in, w_qb, w_kvb):
    i1 = MLA_Q_RANK
    i2 = i1 + MLA_KV_RANK
    i3 = i2 + MLA_ROPE_DIM
    half = MLA_ROPE_DIM // 2
    wk1, wk2 = w_in[:, i2:i2 + half], w_in[:, i2 + half:i3]
    w1 = jnp.concatenate([w_in[:, :i2], wk1, wk2, wk1, wk2, wk2, wk1, wk2, wk1, w_in[:, i3:]], axis=1)
    wq = w_qb.reshape(MLA_Q_RANK, N_HEADS, MLA_NOPE_DIM + MLA_ROPE_DIM)
    nope = wq[:, :, :MLA_NOPE_DIM].reshape(MLA_Q_RANK, -1)
    r1 = wq[:, :, MLA_NOPE_DIM:MLA_NOPE_DIM + half]
    r2 = wq[:, :, MLA_NOPE_DIM + half:]
    rope_a = jnp.concatenate([r1, r2], axis=2).reshape(MLA_Q_RANK, -1)
    rope_b = jnp.concatenate([r2, r1], axis=2).reshape(MLA_Q_RANK, -1)
    wq2 = jnp.concatenate([nope, rope_a, rope_b], axis=1)
    wkv = w_kvb.reshape(MLA_KV_RANK, N_HEADS, 2 * LANES)
    wk = wkv[:, :, :MLA_NOPE_DIM].reshape(MLA_KV_RANK, -1)
    wvt = wkv[:, :, MLA_NOPE_DIM:].reshape(MLA_KV_RANK, -1).T
    return w1.astype(BF16), wq2.astype(BF16), wk.astype(BF16), wvt.astype(BF16)


def _mla_proj(x2d, positions, ln, w_in, q_norm, w_qb, kv_norm, w_kvb, tokens_per_batch):
    m, d = x2d.shape
    tm = min(256, tokens_per_batch)
    w1, wq2, wk, wvt = _mla_weights(w_in, w_qb, w_kvb)
    half = MLA_ROPE_DIM // 2
    inv_freq = ROPE_BASE ** (-jnp.arange(0, MLA_ROPE_DIM, 2, dtype=F32) / MLA_ROPE_DIM)
    inv4 = jnp.tile(inv_freq, LANES // half).reshape(1, LANES)
    pos = positions.astype(F32).reshape(m, 1)
    return pl.pallas_call(
        _mla_proj_kernel,
        grid=(m // tm,),
        in_specs=[pl.BlockSpec((tm, d), lambda i: (i, 0)),
                  pl.BlockSpec((tm, 1), lambda i: (i, 0)),
                  _resident((1, LANES)), _resident((1, d)), _resident(w1.shape),
                  _resident((1, MLA_Q_RANK)), _resident(wq2.shape),
                  _resident((1, MLA_KV_RANK)), _resident(wk.shape), _resident(wvt.shape)],
        out_specs=[_head_block(tm, 2 * LANES), _head_block(tm, 2 * LANES),
                   pl.BlockSpec((N_HEADS, LANES, tm), lambda i: (0, 0, i)), _head_block(tm, LANES)],
        out_shape=[_head_major(m, 2 * LANES, BF16), _head_major(m, 2 * LANES, BF16),
                   jax.ShapeDtypeStruct((N_HEADS, LANES, m), BF16), _head_major(m, LANES, F32)],
        compiler_params=_params("parallel"),
        name="mla_proj",
    )(x2d, pos, inv4, ln.reshape(1, d), w1, q_norm.reshape(1, -1), wq2, kv_norm.reshape(1, -1), wk, wvt)


def _out_kernel(o_ref, w_ref, x_ref, *rest, final):
    if final:
        g_ref, y_ref = rest
    else:
        (y_ref,) = rest
    o = jnp.concatenate([o_ref[hh] for hh in range(N_HEADS)], axis=1)
    y = x_ref[...] + _dot(o, w_ref[...])
    if final:
        y = _rmsnorm(y, g_ref[...])
    y_ref[...] = y


def _out_proj(o, w_out, x2d, final_g=None):
    m, d = x2d.shape
    tm = min(512, m)
    final = final_g is not None
    in_specs = [_head_block(tm, LANES), _resident((D_INNER, d)), pl.BlockSpec((tm, d), lambda i: (i, 0))]
    args = [o, w_out, x2d]
    if final:
        in_specs.append(_resident((1, d)))
        args.append(final_g.reshape(1, d))
    return pl.pallas_call(
        functools.partial(_out_kernel, final=final),
        grid=(m // tm,),
        in_specs=in_specs,
        out_specs=pl.BlockSpec((tm, d), lambda i: (i, 0)),
        out_shape=jax.ShapeDtypeStruct((m, d), F32),
        compiler_params=_params("parallel"),
        name="out_proj_final" if final else "out_proj",
    )(*args)


def _stick_kernel(q_ref, k_ref, v_ref, g_ref, suffix_ref, o_ref, *, t):
    i = pl.program_id(2)
    n_heads = q_ref.shape[0]
    row = lax.broadcasted_iota(jnp.int32, (t, t), 0)
    col = lax.broadcasted_iota(jnp.int32, (t, t), 1)
    strict = col < row

    def keys(kj):
        return pl.ds(pl.multiple_of(kj * t, t), t)

    def skip_bits(z):
        return jnp.maximum(z, 0.0) + jnp.log2(1.0 + jnp.exp2(-jnp.abs(z)))

    def suffix_sums(u):
        return _dot(jnp.concatenate(_bf16_terms(u, 2), axis=1), suffix_ref[...])

    def row_sum(u):
        return jnp.sum(u, axis=1, keepdims=True)

    prev = jnp.maximum(i - 1, 0)
    heads = range(n_heads)
    z_diag, z_prev, u_diag, u_prev = [], [], [], []
    for hh in heads:
        z_diag.append(_dot_nt(q_ref[hh], k_ref[hh, keys(i), :]))
        z_prev.append(_dot_nt(q_ref[hh], k_ref[hh, keys(prev), :]))
    for hh in heads:
        u_diag.append(jnp.where(strict, skip_bits(z_diag[hh]), 0.0))
        u_prev.append(skip_bits(z_prev[hh]))
    later_diag = [suffix_sums(u_diag[hh]) for hh in heads]
    later_prev = [suffix_sums(u_prev[hh]) for hh in heads]
    state = []
    for hh in heads:
        w = jnp.where(strict, jnp.exp2(z_diag[hh] - later_diag[hh]), 0.0)
        acc = _dot(w.astype(BF16), v_ref[hh, keys(i), :])
        c = -row_sum(u_diag[hh])
        w = jnp.exp2(z_prev[hh] - later_prev[hh] + c)
        vb = v_ref[hh, keys(prev), :]
        acc = acc + _dot(w.astype(BF16), jnp.where(i > 0, vb, jnp.zeros_like(vb)))
        state.append((c - row_sum(u_prev[hh]), acc))
    state = tuple(state)

    def more(carry):
        kj, state = carry
        c_max = functools.reduce(jnp.maximum, [jnp.max(c) for c, _ in state])
        return jnp.logical_and(kj >= 0, c_max > -UNDERFLOW_LOG2)

    def body(carry):
        kj, state = carry
        new_state = []
        for hh, (c, acc) in enumerate(state):
            z = _dot_nt(q_ref[hh], k_ref[hh, keys(kj), :])
            u = skip_bits(z)
            w = jnp.exp2(z - suffix_sums(u) + c)
            acc = acc + _dot(w.astype(BF16), v_ref[hh, keys(kj), :])
            new_state.append((c - row_sum(u), acc))
        return kj - 1, tuple(new_state)

    _, state = lax.while_loop(more, body, (i - 2, state))
    for hh, (_, acc) in enumerate(state):
        o_ref[hh] = (acc * _silu(g_ref[hh])).astype(BF16)


def _stick_attention(qkv, gate, batch, seq):
    t = min(256, seq)
    nq = seq // t
    m = batch * seq
    h = N_HEADS
    hp = STICK_HEADS_PER_STEP
    groups = h // hp
    from_here = jnp.tril(jnp.ones((t, t), BF16))
    suffix_sum = jnp.concatenate([from_here, from_here], axis=0)
    return pl.pallas_call(
        functools.partial(_stick_kernel, t=t),
        grid=(groups, batch, nq),
        in_specs=[
            pl.BlockSpec((hp, t, LANES), lambda hh, b, i: (hh, b * nq + i, 0)),
            pl.BlockSpec((hp, seq, LANES), lambda hh, b, i: (groups + hh, b, 0)),
            pl.BlockSpec((hp, seq, LANES), lambda hh, b, i: (2 * groups + hh, b, 0)),
            pl.BlockSpec((hp, t, LANES), lambda hh, b, i: (hh, b * nq + i, 0)),
            _resident(suffix_sum.shape),
        ],
        out_specs=pl.BlockSpec((hp, t, LANES), lambda hh, b, i: (hh, b * nq + i, 0)),
        out_shape=jax.ShapeDtypeStruct((h, m, LANES), BF16),
        compiler_params=_params("parallel", "parallel", "arbitrary"),
        name="stick_attention",
    )(qkv, qkv, qkv, gate, suffix_sum)


def _softmax_kernel(q_ref, k_ref, vt_ref, g_ref, o_ref, s_ref, p_ref, *, t, chunked):
    i = pl.program_id(2)
    n_heads = q_ref.shape[0]
    key = lax.broadcasted_iota(jnp.int32, (t, t), 0)
    qry = lax.broadcasted_iota(jnp.int32, (t, t), 1)
    allowed = (key // CHUNK) <= (qry // CHUNK) if chunked else key <= qry

    subs = t // KEY_SUB
    top = (i + 1) * subs - 1

    def key_slice(g):
        return pl.ds(pl.multiple_of(g * KEY_SUB, KEY_SUB), KEY_SUB)

    def scores(hh, g, slot):
        s_ref[hh, slot] = _dot_nt(k_ref[hh, key_slice(jnp.maximum(g, 0)), :], q_ref[hh])

    def weighted_values(hh, g, slot):
        return _dot(vt_ref[hh, :, key_slice(g)], p_ref[hh, slot])

    def softmax_step(hh, slot, mask, pv_prev, m_run, l_run, alpha_prev, acc):
        if mask is not None:
            s_ref[hh, slot] = jnp.where(mask, s_ref[hh, slot], NEG)
        m_new = jnp.maximum(m_run, jnp.max(s_ref[hh, slot], axis=0, keepdims=True))
        alpha = jnp.exp2(m_run - m_new)
        p = jnp.exp2(s_ref[hh, slot] - m_new)
        p_ref[hh, slot] = p.astype(BF16)
        l_run = alpha * l_run + jnp.sum(p, axis=0, keepdims=True)
        return m_new, l_run, alpha, alpha_prev * acc + pv_prev

    def tile(first, states, diagonal):
        for j in range(subs):
            g, slot = first - j, j % 2
            mask = None
            if diagonal:
                local = (subs - 1 - j) * KEY_SUB
                mask = allowed[local:local + KEY_SUB]
            for hh in range(n_heads):
                scores(hh, g - 1, 1 - slot)
            pv_prev = [weighted_values(hh, jnp.minimum(g + 1, top), 1 - slot) for hh in range(n_heads)]
            states = tuple(softmax_step(hh, slot, mask, pv_prev[hh], *states[hh]) for hh in range(n_heads))
        return states

    init = (jnp.full((1, t), NEG, F32), jnp.zeros((1, t), F32), jnp.ones((1, t), F32),
            jnp.zeros((HEAD_DIM, t), F32))
    for hh in range(n_heads):
        scores(hh, top, 0)
        p_ref[hh, 1] = jnp.zeros(p_ref.shape[2:], BF16)
    carry = tile(top, (init,) * n_heads, True)

    def body(n, carry):
        return tile(top - subs * (n + 1), carry, False)

    carry = lax.fori_loop(0, i, body, carry)
    for hh in range(n_heads):
        _, l_run, alpha, acc = carry[hh]
        acc = alpha * acc + weighted_values(hh, 0, 1)
        o_ref[hh] = ((acc / l_run).T * _silu(g_ref[hh])).astype(BF16)


def _softmax_attention(qcat, kcat, vt, gate, batch, seq, chunked):
    t = min(1024, seq)
    nq = seq // t
    m = batch * seq
    h = N_HEADS
    width = qcat.shape[-1]
    hp = HEADS_PER_STEP
    return pl.pallas_call(
        functools.partial(_softmax_kernel, t=t, chunked=chunked),
        grid=(h // hp, batch, nq),
        in_specs=[
            pl.BlockSpec((hp, t, width), lambda hh, b, i: (hh, b * nq + i, 0)),
            pl.BlockSpec((hp, seq, width), lambda hh, b, i: (hh, b, 0)),
            pl.BlockSpec((hp, LANES, seq), lambda hh, b, i: (hh, 0, b)),
            pl.BlockSpec((hp, t, LANES), lambda hh, b, i: (hh, b * nq + i, 0)),
        ],
        out_specs=pl.BlockSpec((hp, t, LANES), lambda hh, b, i: (hh, b * nq + i, 0)),
        scratch_shapes=[pltpu.VMEM((hp, 2, min(KEY_SUB, t), t), F32),
                        pltpu.VMEM((hp, 2, min(KEY_SUB, t), t), BF16)],
        out_shape=jax.ShapeDtypeStruct((h, m, LANES), BF16),
        compiler_params=_params("parallel", "parallel", "arbitrary"),
        name="mla_attention" if chunked else "forget_attention",
    )(qcat, kcat, vt, gate)


def kernel(x, positions, ln0, w_in0, w_out0, ln1, w_in1, q_norm1, w_qb1, kv_norm1, w_kvb1, w_out1,
           ln2, w_in2, b_f2, w_out2, ln3, w_in3, w_out3, final_norm):
    batch, seq, d = x.shape
    x2d = x.reshape(batch * seq, d)

    qkv, gate = _norm_proj(x2d, ln0, w_in0.astype(BF16), seq)
    o = _stick_attention(qkv, gate, batch, seq)
    x2d = _out_proj(o, w_out0.astype(BF16), x2d)

    qcat, kcat, vt, gate = _mla_proj(x2d, positions, ln1, w_in1, q_norm1, w_qb1, kv_norm1, w_kvb1, seq)
    o = _softmax_attention(qcat, kcat, vt, gate, batch, seq, chunked=True)
    x2d = _out_proj(o, w_out1.astype(BF16), x2d)

    qcat, kcat, vt, gate = _forget_proj(x2d, ln2, w_in2, b_f2, seq)
    o = _softmax_attention(qcat, kcat, vt, gate, batch, seq, chunked=False)
    x2d = _out_proj(o, w_out2.astype(BF16), x2d)

    qkv, gate = _norm_proj(x2d, ln3, w_in3.astype(BF16), seq)
    o = _stick_attention(qkv, gate, batch, seq)
    x2d = _out_proj(o, w_out3.astype(BF16), x2d, final_g=final_norm)
    return x2d.reshape(batch, seq, d)
```

```python
import functools

import jax
import jax.numpy as jnp
from jax import lax
from jax.experimental import pallas as pl
from jax.experimental.pallas import tpu as pltpu

D_MODEL = 1024
N_HEADS = 16
HEAD_DIM = 128
D_INNER = N_HEADS * HEAD_DIM
CHUNK = 64
MLA_Q_RANK = 256
MLA_KV_RANK = 128
MLA_NOPE_DIM = 128
MLA_ROPE_DIM = 64
ROPE_BASE = 10000.0
EPS = 1e-6
NEG = -1e30
LOG2E = 1.4426950408889634
UNDERFLOW_LOG2 = 150.0

HEADS_PER_STEP = 2
STICK_HEADS_PER_STEP = 8
KEY_SUB = 512
LANES = 128
VMEM_LIMIT_BYTES = 56 * 1024 * 1024

F32 = jnp.float32
BF16 = jnp.bfloat16


def _params(*semantics):
    return pltpu.CompilerParams(dimension_semantics=semantics,
                                vmem_limit_bytes=VMEM_LIMIT_BYTES)


def _resident(shape):
    zeros = (0,) * len(shape)
    return pl.BlockSpec(shape, lambda *_: zeros, pipeline_mode=pl.Buffered(1))


def _rmsnorm(x, g):
    return x * lax.rsqrt(jnp.mean(x * x, axis=-1, keepdims=True) + EPS) * g


def _dot(a, b):
    return jnp.dot(a, b, preferred_element_type=F32)


def _dot_nt(a, b):
    return lax.dot_general(a, b, (((1,), (1,)), ((), ())), preferred_element_type=F32)


def _bf16_terms(a, terms):
    out = []
    for t in range(terms):
        part = a.astype(BF16)
        out.append(part)
        if t + 1 < terms:
            a = a - part.astype(F32)
    return out


def _silu(g):
    return g * (1.0 / (1.0 + jnp.exp(-g)))


def _head_major(m, width, dtype):
    return jax.ShapeDtypeStruct((N_HEADS, m, width), dtype)


def _head_block(tm, width):
    return pl.BlockSpec((N_HEADS, tm, width), lambda i: (0, i, 0))


def _proj_kernel(x_ref, g_ref, w_ref, qkv_ref, gate_ref):
    h = _rmsnorm(x_ref[...], g_ref[...]).astype(BF16)
    n_bf16 = 3 * N_HEADS
    step = 8
    for c0 in range(0, 4 * N_HEADS, step):
        acc = _dot(h, w_ref[:, c0 * LANES:(c0 + step) * LANES])
        for c in range(step):
            piece = acc[:, c * LANES:(c + 1) * LANES]
            if c0 + c < N_HEADS:
                qkv_ref[c0 + c] = (piece * (HEAD_DIM ** -0.5 * LOG2E)).astype(BF16)
            elif c0 + c < n_bf16:
                qkv_ref[c0 + c] = piece.astype(BF16)
            else:
                gate_ref[c0 + c - n_bf16] = piece


def _norm_proj(x2d, g, w, tokens_per_batch):
    m, d = x2d.shape
    tm = min(256, tokens_per_batch)
    n_bf16 = 3 * N_HEADS
    return pl.pallas_call(
        _proj_kernel,
        grid=(m // tm,),
        in_specs=[pl.BlockSpec((tm, d), lambda i: (i, 0)), _resident((1, d)), _resident(w.shape)],
        out_specs=[pl.BlockSpec((n_bf16, tm, LANES), lambda i: (0, i, 0)), _head_block(tm, LANES)],
        out_shape=[jax.ShapeDtypeStruct((n_bf16, m, LANES), BF16), _head_major(m, LANES, F32)],
        compiler_params=_params("parallel"),
        name="norm_proj",
    )(x2d, g.reshape(1, d), w)


def _forget_proj_kernel(x_ref, g_ref, w_ref, wvt_ref, wf_ref, bf_ref,
                        qcat_ref, kcat_ref, vt_ref, gate_ref, carry_ref, *, per_batch):
    i = pl.program_id(0)
    tm = x_ref.shape[0]
    h = _rmsnorm(x_ref[...], g_ref[...]).astype(BF16)

    @pl.when(i % per_batch == 0)
    def _():
        carry_ref[...] = jnp.zeros_like(carry_ref)

    f = _dot(h, wf_ref[...]) + bf_ref[...]
    log_f = (jnp.minimum(f, 0.0) - jnp.log1p(jnp.exp(-jnp.abs(f)))) * LOG2E
    row = lax.broadcasted_iota(jnp.int32, (tm, tm), 0)
    col = lax.broadcasted_iota(jnp.int32, (tm, tm), 1)
    up_to = (col <= row).astype(BF16)
    cf = carry_ref[0:1, :]
    for part in _bf16_terms(log_f, 3):
        cf = cf + _dot(up_to, part)
    carry_ref[...] = jnp.broadcast_to(cf[tm - 1:tm, :], carry_ref.shape)

    hi, mid, lo = (part.astype(F32) for part in _bf16_terms(cf, 3))
    lane = lax.broadcasted_iota(jnp.int32, (1, LANES), 1)
    for hh in range(N_HEADS):
        terms = jnp.where(lane % 3 == 0, hi[:, hh:hh + 1],
                          jnp.where(lane % 3 == 1, mid[:, hh:hh + 1], lo[:, hh:hh + 1]))
        q_ext = jnp.where(lane < 3, terms, jnp.where(lane < 6, 1.0, 0.0))
        k_ext = jnp.where(lane < 3, 1.0, jnp.where(lane < 6, -terms, 0.0))
        qcat_ref[hh, :, LANES:] = q_ext.astype(BF16)
        kcat_ref[hh, :, LANES:] = k_ext.astype(BF16)

    step = 8
    for c0 in range(0, 3 * N_HEADS, step):
        acc = _dot(h, w_ref[:, c0 * LANES:(c0 + step) * LANES])
        for c in range(step):
            piece = acc[:, c * LANES:(c + 1) * LANES]
            g_idx = c0 + c
            if g_idx < N_HEADS:
                qcat_ref[g_idx, :, :LANES] = (piece * (HEAD_DIM ** -0.5 * LOG2E)).astype(BF16)
            elif g_idx < 2 * N_HEADS:
                kcat_ref[g_idx - N_HEADS, :, :LANES] = piece.astype(BF16)
            else:
                gate_ref[g_idx - 2 * N_HEADS] = piece
    for c0 in range(0, N_HEADS, step):
        vt = _dot_nt(wvt_ref[c0 * LANES:(c0 + step) * LANES, :], h)
        for c in range(step):
            vt_ref[c0 + c] = vt[c * LANES:(c + 1) * LANES, :].astype(BF16)


def _forget_proj(x2d, g, w_in, b_f, tokens_per_batch):
    m, d = x2d.shape
    tm = min(256, tokens_per_batch)
    di = D_INNER
    w = jnp.concatenate([w_in[:, :2 * di], w_in[:, 3 * di:4 * di]], axis=1).astype(BF16)
    wvt = w_in[:, 2 * di:3 * di].T.astype(BF16)
    wf = jnp.pad(w_in[:, 4 * di:], ((0, 0), (0, LANES - N_HEADS))).astype(BF16)
    bf = jnp.pad(b_f, (0, LANES - N_HEADS)).reshape(1, LANES)
    return pl.pallas_call(
        functools.partial(_forget_proj_kernel, per_batch=tokens_per_batch // tm),
        grid=(m // tm,),
        in_specs=[pl.BlockSpec((tm, d), lambda i: (i, 0)), _resident((1, d)), _resident(w.shape),
                  _resident(wvt.shape), _resident(wf.shape), _resident(bf.shape)],
        out_specs=[_head_block(tm, 2 * LANES), _head_block(tm, 2 * LANES),
                   pl.BlockSpec((N_HEADS, LANES, tm), lambda i: (0, 0, i)), _head_block(tm, LANES)],
        out_shape=[_head_major(m, 2 * LANES, BF16), _head_major(m, 2 * LANES, BF16),
                   jax.ShapeDtypeStruct((N_HEADS, LANES, m), BF16), _head_major(m, LANES, F32)],
        scratch_shapes=[pltpu.VMEM((8, LANES), F32)],
        compiler_params=_params("arbitrary"),
        name="forget_proj",
    )(x2d, g.reshape(1, d), w, wvt, wf, bf)


def _mla_proj_kernel(x_ref, pos_ref, inv_ref, g_ref, w1_ref, qn_ref, wq_ref, kvn_ref, wk_ref, wvt_ref,
                     qcat_ref, kcat_ref, vt_ref, gate_ref):
    h = _rmsnorm(x_ref[...], g_ref[...]).astype(BF16)
    lat = _dot(h, w1_ref[:, :5 * LANES])
    q_lat = lat[:, :MLA_Q_RANK]
    kv_lat = lat[:, MLA_Q_RANK:MLA_Q_RANK + MLA_KV_RANK]
    k_a = lat[:, 3 * LANES:4 * LANES]
    k_b = lat[:, 4 * LANES:5 * LANES]

    lane = lax.broadcasted_iota(jnp.int32, (1, LANES), 1)
    half = MLA_ROPE_DIM // 2
    ang = pos_ref[...] * inv_ref[...]
    cos = jnp.cos(ang)
    sin = jnp.where((lane // half) % 2 == 0, -1.0, 1.0) * jnp.sin(ang)
    k_rope = (k_a * cos + k_b * sin).astype(BF16)

    kvn = _rmsnorm(kv_lat, kvn_ref[...]).astype(BF16)
    step = 8
    for c0 in range(0, N_HEADS, step):
        k_nope = _dot(kvn, wk_ref[:, c0 * LANES:(c0 + step) * LANES])
        vt = _dot_nt(wvt_ref[c0 * LANES:(c0 + step) * LANES, :], kvn)
        for c in range(step):
            kcat_ref[c0 + c] = jnp.concatenate(
                [k_nope[:, c * LANES:(c + 1) * LANES].astype(BF16), k_rope], axis=1)
            vt_ref[c0 + c] = vt[c * LANES:(c + 1) * LANES, :].astype(BF16)

    qn = _rmsnorm(q_lat, qn_ref[...]).astype(BF16)
    n_pairs = N_HEADS // 2
    rope_a = _dot(qn, wq_ref[:, N_HEADS * LANES:(N_HEADS + n_pairs) * LANES])
    rope_b = _dot(qn, wq_ref[:, (N_HEADS + n_pairs) * LANES:])
    first_head = lane < MLA_ROPE_DIM
    q_scale = (MLA_NOPE_DIM + MLA_ROPE_DIM) ** -0.5 * LOG2E
    for c0 in range(0, N_HEADS, step):
        nope = _dot(qn, wq_ref[:, c0 * LANES:(c0 + step) * LANES])
        for c in range(step):
            hh = c0 + c
            pair = hh // 2
            sl = slice(pair * LANES, (pair + 1) * LANES)
            q_rope = rope_a[:, sl] * cos + rope_b[:, sl] * sin
            mine = first_head if hh % 2 == 0 else jnp.logical_not(first_head)
            q_rope = jnp.where(mine, q_rope * q_scale, 0.0)
            q_nope = nope[:, c * LANES:(c + 1) * LANES] * q_scale
            qcat_ref[hh] = jnp.concatenate([q_nope.astype(BF16), q_rope.astype(BF16)], axis=1)

    for c0 in range(0, N_HEADS, step):
        gt = _dot(h, w1_ref[:, (5 + c0) * LANES:(5 + c0 + step) * LANES])
        for c in range(step):
            gate_ref[c0 + c] = gt[:, c * LANES:(c + 1) * LANES]


def _mla_weights(w_in, w_qb, w_kvb):
    i1 = MLA_Q_RANK
    i2 = i1 + MLA_KV_RANK
    i3 = i2 + MLA_ROPE_DIM
    half = MLA_ROPE_DIM // 2
    wk1, wk2 = w_in[:, i2:i2 + half], w_in[:, i2 + half:i3]
    w1 = jnp.concatenate([w_in[:, :i2], wk1, wk2, wk1, wk2, wk2, wk1, wk2, wk1, w_in[:, i3:]], axis=1)
    wq = w_qb.reshape(MLA_Q_RANK, N_HEADS, MLA_NOPE_DIM + MLA_ROPE_DIM)
    nope = wq[:, :, :MLA_NOPE_DIM].reshape(MLA_Q_RANK, -1)
    r1 = wq[:, :, MLA_NOPE_DIM:MLA_NOPE_DIM + half]
    r2 = wq[:, :, MLA_NOPE_DIM + half:]
    rope_a = jnp.concatenate([r1, r2], axis=2).reshape(MLA_Q_RANK, -1)
    rope_b = jnp.concatenate([r2, r1], axis=2).reshape(MLA_Q_RANK, -1)
    wq2 = jnp.concatenate([nope, rope_a, rope_b], axis=1)
    wkv = w_kvb.reshape(MLA_KV_RANK, N_HEADS, 2 * LANES)
    wk = wkv[:, :, :MLA_NOPE_DIM].reshape(MLA_KV_RANK, -1)
    wvt = wkv[:, :, MLA_NOPE_DIM:].reshape(MLA_KV_RANK, -1).T
    return w1.astype(BF16), wq2.astype(BF16), wk.astype(BF16), wvt.astype(BF16)


def _mla_proj(x2d, positions, ln, w_in, q_norm, w_qb, kv_norm, w_kvb, tokens_per_batch):
    m, d = x2d.shape
    tm = min(256, tokens_per_batch)
    w1, wq2, wk, wvt = _mla_weights(w_in, w_qb, w_kvb)
    half = MLA_ROPE_DIM // 2
    inv_freq = ROPE_BASE ** (-jnp.arange(0, MLA_ROPE_DIM, 2, dtype=F32) / MLA_ROPE_DIM)
    inv4 = jnp.tile(inv_freq, LANES // half).reshape(1, LANES)
    pos = positions.astype(F32).reshape(m, 1)
    return pl.pallas_call(
        _mla_proj_kernel,
        grid=(m // tm,),
        in_specs=[pl.BlockSpec((tm, d), lambda i: (i, 0)),
                  pl.BlockSpec((tm, 1), lambda i: (i, 0)),
                  _resident((1, LANES)), _resident((1, d)), _resident(w1.shape),
                  _resident((1, MLA_Q_RANK)), _resident(wq2.shape),
                  _resident((1, MLA_KV_RANK)), _resident(wk.shape), _resident(wvt.shape)],
        out_specs=[_head_block(tm, 2 * LANES), _head_block(tm, 2 * LANES),
                   pl.BlockSpec((N_HEADS, LANES, tm), lambda i: (0, 0, i)), _head_block(tm, LANES)],
        out_shape=[_head_major(m, 2 * LANES, BF16), _head_major(m, 2 * LANES, BF16),
                   jax.ShapeDtypeStruct((N_HEADS, LANES, m), BF16), _head_major(m, LANES, F32)],
        compiler_params=_params("parallel"),
        name="mla_proj",
    )(x2d, pos, inv4, ln.reshape(1, d), w1, q_norm.reshape(1, -1), wq2, kv_norm.reshape(1, -1), wk, wvt)


def _out_kernel(o_ref, w_ref, x_ref, *rest, final):
    if final:
        g_ref, y_ref = rest
    else:
        (y_ref,) = rest
    o = jnp.concatenate([o_ref[hh] for hh in range(N_HEADS)], axis=1)
    y = x_ref[...] + _dot(o, w_ref[...])
    if final:
        y = _rmsnorm(y, g_ref[...])
    y_ref[...] = y


def _out_proj(o, w_out, x2d, final_g=None):
    m, d = x2d.shape
    tm = min(512, m)
    final = final_g is not None
    in_specs = [_head_block(tm, LANES), _resident((D_INNER, d)), pl.BlockSpec((tm, d), lambda i: (i, 0))]
    args = [o, w_out, x2d]
    if final:
        in_specs.append(_resident((1, d)))
        args.append(final_g.reshape(1, d))
    return pl.pallas_call(
        functools.partial(_out_kernel, final=final),
        grid=(m // tm,),
        in_specs=in_specs,
        out_specs=pl.BlockSpec((tm, d), lambda i: (i, 0)),
        out_shape=jax.ShapeDtypeStruct((m, d), F32),
        compiler_params=_params("parallel"),
        name="out_proj_final" if final else "out_proj",
    )(*args)


def _stick_kernel(q_ref, k_ref, v_ref, g_ref, suffix_ref, o_ref, acc_ref, c_ref, *, t):
    i = pl.program_id(2)
    n_heads = q_ref.shape[0]
    row = lax.broadcasted_iota(jnp.int32, (t, t), 0)
    col = lax.broadcasted_iota(jnp.int32, (t, t), 1)
    strict = col < row

    def keys(kj):
        return pl.ds(pl.multiple_of(kj * t, t), t)

    def skip_bits(z):
        return jnp.maximum(z, 0.0) + jnp.log2(1.0 + jnp.exp2(-jnp.abs(z)))

    def suffix_sums(u):
        return _dot(jnp.concatenate(_bf16_terms(u, 2), axis=1), suffix_ref[...])

    def row_sum(u):
        return jnp.sum(u, axis=1, keepdims=True)

    prev = jnp.maximum(i - 1, 0)
    heads = range(n_heads)
    z_diag, z_prev, u_diag, u_prev = [], [], [], []
    for hh in heads:
        z_diag.append(_dot_nt(q_ref[hh], k_ref[hh, keys(i), :]))
        z_prev.append(_dot_nt(q_ref[hh], k_ref[hh, keys(prev), :]))
    for hh in heads:
        u_diag.append(jnp.where(strict, skip_bits(z_diag[hh]), 0.0))
        u_prev.append(skip_bits(z_prev[hh]))
    later_diag = [suffix_sums(u_diag[hh]) for hh in heads]
    later_prev = [suffix_sums(u_prev[hh]) for hh in heads]
    state = []
    for hh in heads:
        w = jnp.where(strict, jnp.exp2(z_diag[hh] - later_diag[hh]), 0.0)
        acc = _dot(w.astype(BF16), v_ref[hh, keys(i), :])
        c = -row_sum(u_diag[hh])
        w = jnp.exp2(z_prev[hh] - later_prev[hh] + c)
        vb = v_ref[hh, keys(prev), :]
        acc = acc + _dot(w.astype(BF16), jnp.where(i > 0, vb, jnp.zeros_like(vb)))
        c = c - row_sum(u_prev[hh])
        acc_ref[hh] = acc
        c_ref[hh] = c
        state.append(jnp.max(c))
    c_max = functools.reduce(jnp.maximum, state)

    def more(carry):
        kj, c_max = carry
        return jnp.logical_and(kj >= 0, c_max > -UNDERFLOW_LOG2)

    def body(carry):
        kj, _ = carry
        c_maxes = []
        for hh in heads:
            z = _dot_nt(q_ref[hh], k_ref[hh, keys(kj), :])
            u = skip_bits(z)
            w = jnp.exp2(z - suffix_sums(u) + c_ref[hh])
            acc_ref[hh] += _dot(w.astype(BF16), v_ref[hh, keys(kj), :])
            c = c_ref[hh] - row_sum(u)
            c_ref[hh] = c
            c_maxes.append(jnp.max(c))
        return kj - 1, functools.reduce(jnp.maximum, c_maxes)

    lax.while_loop(more, body, (i - 2, c_max))
    for hh in heads:
        o_ref[hh] = (acc_ref[hh] * _silu(g_ref[hh])).astype(BF16)


def _stick_attention(qkv, gate, batch, seq):
    t = min(256, seq)
    nq = seq // t
    m = batch * seq
    h = N_HEADS
    hp = STICK_HEADS_PER_STEP
    groups = h // hp
    from_here = jnp.tril(jnp.ones((t, t), BF16))
    suffix_sum = jnp.concatenate([from_here, from_here], axis=0)
    return pl.pallas_call(
        functools.partial(_stick_kernel, t=t),
        grid=(groups, batch, nq),
        in_specs=[
            pl.BlockSpec((hp, t, LANES), lambda hh, b, i: (hh, b * nq + i, 0)),
            pl.BlockSpec((hp, seq, LANES), lambda hh, b, i: (groups + hh, b, 0)),
            pl.BlockSpec((hp, seq, LANES), lambda hh, b, i: (2 * groups + hh, b, 0)),
            pl.BlockSpec((hp, t, LANES), lambda hh, b, i: (hh, b * nq + i, 0)),
            _resident(suffix_sum.shape),
        ],
        out_specs=pl.BlockSpec((hp, t, LANES), lambda hh, b, i: (hh, b * nq + i, 0)),
        out_shape=jax.ShapeDtypeStruct((h, m, LANES), BF16),
        scratch_shapes=[pltpu.VMEM((hp, t, HEAD_DIM), F32), pltpu.VMEM((hp, t, 1), F32)],
        compiler_params=_params("parallel", "parallel", "arbitrary"),
        name="stick_attention",
    )(qkv, qkv, qkv, gate, suffix_sum)


def _softmax_kernel(q_ref, k_ref, vt_ref, g_ref, o_ref, s_ref, p_ref, *, t, chunked):
    n_heads, seq, _ = q_ref.shape
    heads = range(n_heads)
    key = lax.broadcasted_iota(jnp.int32, (t, t), 0)
    qry = lax.broadcasted_iota(jnp.int32, (t, t), 1)
    allowed = (key // CHUNK) <= (qry // CHUNK) if chunked else key <= qry

    subs = t // KEY_SUB

    def key_slice(g):
        if isinstance(g, int):
            return pl.ds(max(g, 0) * KEY_SUB, KEY_SUB)
        return pl.ds(pl.multiple_of(jnp.maximum(g, 0) * KEY_SUB, KEY_SUB), KEY_SUB)

    def softmax_step(hh, slot, mask, pv_prev, m_run, l_run, alpha_prev, acc):
        s = s_ref[hh, slot]
        s_max = jnp.max(s if mask is None else jnp.where(mask, s, NEG), axis=0, keepdims=True)
        m_new = jnp.maximum(m_run, s_max)
        alpha = jnp.exp2(m_run - m_new)
        p = jnp.exp2(s_ref[hh, slot] - m_new)
        if mask is not None:
            p = jnp.where(mask, p, 0.0)
        p_ref[hh, slot] = p.astype(BF16)
        l_run = alpha * l_run + jnp.sum(p, axis=0, keepdims=True)
        if pv_prev is not None:
            acc = alpha_prev * acc + pv_prev
        return m_new, l_run, alpha, acc

    for qi in range(seq // t):
        rows = slice(qi * t, (qi + 1) * t)
        top = (qi + 1) * subs - 1

        def scores(hh, g, slot):
            s_ref[hh, slot] = _dot_nt(k_ref[hh, key_slice(g), :], q_ref[hh, rows, :])

        def weighted_values(hh, g, slot):
            return _dot(vt_ref[hh, :, key_slice(g)], p_ref[hh, slot])

        def tile(first, states, diagonal):
            for j in range(subs):
                g, slot = first - j, j % 2
                mask = None
                if diagonal:
                    local = (subs - 1 - j) * KEY_SUB
                    mask = allowed[local:local + KEY_SUB]
                for hh in heads:
                    scores(hh, g - 1, 1 - slot)
                if diagonal and j == 0:
                    pv_prev = [None] * n_heads
                else:
                    pv_prev = [weighted_values(hh, g + 1, 1 - slot) for hh in heads]
                states = tuple(softmax_step(hh, slot, mask, pv_prev[hh], *states[hh]) for hh in heads)
            return states

        for hh in heads:
            scores(hh, top, 0)

        init = (jnp.full((1, t), NEG, F32), jnp.zeros((1, t), F32), jnp.ones((1, t), F32),
                jnp.zeros((HEAD_DIM, t), F32))
        states = tile(top, (init,) * n_heads, True)
        if qi:
            states = lax.fori_loop(0, qi, lambda n, st: tile(top - subs * (n + 1), st, False), states)
        for hh in heads:
            _, l_run, alpha, acc = states[hh]
            acc = alpha * acc + weighted_values(hh, 0, 1)
            o_ref[hh, rows, :] = ((acc * (1.0 / l_run)).T * _silu(g_ref[hh, rows, :])).astype(BF16)


def _softmax_attention(qcat, kcat, vt, gate, batch, seq, chunked):
    t = min(1024, seq)
    m = batch * seq
    h = N_HEADS
    width = qcat.shape[-1]
    hp = HEADS_PER_STEP
    tokens = lambda w: pl.BlockSpec((hp, seq, w), lambda hh, b: (hh, b, 0))
    return pl.pallas_call(
        functools.partial(_softmax_kernel, t=t, chunked=chunked),
        grid=(h // hp, batch),
        in_specs=[tokens(width), tokens(width),
                  pl.BlockSpec((hp, LANES, seq), lambda hh, b: (hh, 0, b)), tokens(LANES)],
        out_specs=tokens(LANES),
        scratch_shapes=[pltpu.VMEM((hp, 2, min(KEY_SUB, t), t), F32),
                        pltpu.VMEM((hp, 2, min(KEY_SUB, t), t), BF16)],
        out_shape=jax.ShapeDtypeStruct((h, m, LANES), BF16),
        compiler_params=_params("parallel", "parallel"),
        name="mla_attention" if chunked else "forget_attention",
    )(qcat, kcat, vt, gate)


def kernel(x, positions, ln0, w_in0, w_out0, ln1, w_in1, q_norm1, w_qb1, kv_norm1, w_kvb1, w_out1,
           ln2, w_in2, b_f2, w_out2, ln3, w_in3, w_out3, final_norm):
    batch, seq, d = x.shape
    x2d = x.reshape(batch * seq, d)

    qkv, gate = _norm_proj(x2d, ln0, w_in0.astype(BF16), seq)
    o = _stick_attention(qkv, gate, batch, seq)
    x2d = _out_proj(o, w_out0.astype(BF16), x2d)

    qcat, kcat, vt, gate = _mla_proj(x2d, positions, ln1, w_in1, q_norm1, w_qb1, kv_norm1, w_kvb1, seq)
    o = _softmax_attention(qcat, kcat, vt, gate, batch, seq, chunked=True)
    x2d = _out_proj(o, w_out1.astype(BF16), x2d)

    qcat, kcat, vt, gate = _forget_proj(x2d, ln2, w_in2, b_f2, seq)
    o = _softmax_attention(qcat, kcat, vt, gate, batch, seq, chunked=False)
    x2d = _out_proj(o, w_out2.astype(BF16), x2d)

    qkv, gate = _norm_proj(x2d, ln3, w_in3.astype(BF16), seq)
    o = _stick_attention(qkv, gate, batch, seq)
    x2d = _out_proj(o, w_out3.astype(BF16), x2d, final_g=final_norm)
    return x2d.reshape(batch, seq, d)
```

```python
import functools

import jax
import jax.numpy as jnp
from jax import lax
from jax.experimental import pallas as pl
from jax.experimental.pallas import tpu as pltpu

D_MODEL = 1024
N_HEADS = 16
HEAD_DIM = 128
D_INNER = N_HEADS * HEAD_DIM
CHUNK = 64
MLA_Q_RANK = 256
MLA_KV_RANK = 128
MLA_NOPE_DIM = 128
MLA_ROPE_DIM = 64
ROPE_BASE = 10000.0
EPS = 1e-6
NEG = -1e30
LOG2E = 1.4426950408889634
UNDERFLOW_LOG2 = 150.0

HEADS_PER_STEP = 2
STICK_HEADS_PER_STEP = 8
KEY_SUB = 512
LANES = 128
VMEM_LIMIT_BYTES = 56 * 1024 * 1024

F32 = jnp.float32
BF16 = jnp.bfloat16


def _params(*semantics):
    return pltpu.CompilerParams(dimension_semantics=semantics,
                                vmem_limit_bytes=VMEM_LIMIT_BYTES)


def _resident(shape):
    zeros = (0,) * len(shape)
    return pl.BlockSpec(shape, lambda *_: zeros, pipeline_mode=pl.Buffered(1))


def _rmsnorm(x, g):
    return x * lax.rsqrt(jnp.mean(x * x, axis=-1, keepdims=True) + EPS) * g


def _dot(a, b):
    return jnp.dot(a, b, preferred_element_type=F32)


def _dot_nt(a, b):
    return lax.dot_general(a, b, (((1,), (1,)), ((), ())), preferred_element_type=F32)


def _bf16_terms(a, terms):
    out = []
    for t in range(terms):
        part = a.astype(BF16)
        out.append(part)
        if t + 1 < terms:
            a = a - part.astype(F32)
    return out


def _silu(g):
    return g * (1.0 / (1.0 + jnp.exp(-g)))


def _head_major(m, width, dtype):
    return jax.ShapeDtypeStruct((N_HEADS, m, width), dtype)


def _head_block(tm, width):
    return pl.BlockSpec((N_HEADS, tm, width), lambda i: (0, i, 0))


def _proj_kernel(x_ref, g_ref, w_ref, qkv_ref, gate_ref):
    h = _rmsnorm(x_ref[...], g_ref[...]).astype(BF16)
    n_bf16 = 3 * N_HEADS
    step = 8
    for c0 in range(0, 4 * N_HEADS, step):
        acc = _dot(h, w_ref[:, c0 * LANES:(c0 + step) * LANES])
        for c in range(step):
            piece = acc[:, c * LANES:(c + 1) * LANES]
            if c0 + c < N_HEADS:
                qkv_ref[c0 + c] = (piece * (HEAD_DIM ** -0.5 * LOG2E)).astype(BF16)
            elif c0 + c < n_bf16:
                qkv_ref[c0 + c] = piece.astype(BF16)
            else:
                gate_ref[c0 + c - n_bf16] = _silu(piece)


def _norm_proj(x2d, g, w, tokens_per_batch):
    m, d = x2d.shape
    tm = min(256, tokens_per_batch)
    n_bf16 = 3 * N_HEADS
    return pl.pallas_call(
        _proj_kernel,
        grid=(m // tm,),
        in_specs=[pl.BlockSpec((tm, d), lambda i: (i, 0)), _resident((1, d)), _resident(w.shape)],
        out_specs=[pl.BlockSpec((n_bf16, tm, LANES), lambda i: (0, i, 0)), _head_block(tm, LANES)],
        out_shape=[jax.ShapeDtypeStruct((n_bf16, m, LANES), BF16), _head_major(m, LANES, F32)],
        compiler_params=_params("parallel"),
        name="norm_proj",
    )(x2d, g.reshape(1, d), w)


def _forget_proj_kernel(x_ref, g_ref, w_ref, wvt_ref, wf_ref, bf_ref,
                        qcat_ref, kcat_ref, vt_ref, gate_ref, carry_ref, *, per_batch):
    i = pl.program_id(0)
    tm = x_ref.shape[0]
    h = _rmsnorm(x_ref[...], g_ref[...]).astype(BF16)

    @pl.when(i % per_batch == 0)
    def _():
        carry_ref[...] = jnp.zeros_like(carry_ref)

    f = _dot(h, wf_ref[...]) + bf_ref[...]
    log_f = (jnp.minimum(f, 0.0) - jnp.log1p(jnp.exp(-jnp.abs(f)))) * LOG2E
    row = lax.broadcasted_iota(jnp.int32, (tm, tm), 0)
    col = lax.broadcasted_iota(jnp.int32, (tm, tm), 1)
    up_to = (col <= row).astype(BF16)
    cf = carry_ref[0:1, :]
    for part in _bf16_terms(log_f, 3):
        cf = cf + _dot(up_to, part)
    carry_ref[...] = jnp.broadcast_to(cf[tm - 1:tm, :], carry_ref.shape)

    hi, mid, lo = (part.astype(F32) for part in _bf16_terms(cf, 3))
    lane = lax.broadcasted_iota(jnp.int32, (1, LANES), 1)
    for hh in range(N_HEADS):
        terms = jnp.where(lane % 3 == 0, hi[:, hh:hh + 1],
                          jnp.where(lane % 3 == 1, mid[:, hh:hh + 1], lo[:, hh:hh + 1]))
        q_ext = jnp.where(lane < 3, terms, jnp.where(lane < 6, 1.0, 0.0))
        k_ext = jnp.where(lane < 3, 1.0, jnp.where(lane < 6, -terms, 0.0))
        qcat_ref[hh, :, LANES:] = q_ext.astype(BF16)
        kcat_ref[hh, :, LANES:] = k_ext.astype(BF16)

    step = 8
    for c0 in range(0, 3 * N_HEADS, step):
        acc = _dot(h, w_ref[:, c0 * LANES:(c0 + step) * LANES])
        for c in range(step):
            piece = acc[:, c * LANES:(c + 1) * LANES]
            g_idx = c0 + c
            if g_idx < N_HEADS:
                qcat_ref[g_idx, :, :LANES] = (piece * (HEAD_DIM ** -0.5 * LOG2E)).astype(BF16)
            elif g_idx < 2 * N_HEADS:
                kcat_ref[g_idx - N_HEADS, :, :LANES] = piece.astype(BF16)
            else:
                gate_ref[g_idx - 2 * N_HEADS] = _silu(piece)
    for c0 in range(0, N_HEADS, step):
        vt = _dot_nt(wvt_ref[c0 * LANES:(c0 + step) * LANES, :], h)
        for c in range(step):
            vt_ref[c0 + c] = vt[c * LANES:(c + 1) * LANES, :].astype(BF16)


def _forget_proj(x2d, g, w_in, b_f, tokens_per_batch):
    m, d = x2d.shape
    tm = min(256, tokens_per_batch)
    di = D_INNER
    w = jnp.concatenate([w_in[:, :2 * di], w_in[:, 3 * di:4 * di]], axis=1).astype(BF16)
    wvt = w_in[:, 2 * di:3 * di].T.astype(BF16)
    wf = jnp.pad(w_in[:, 4 * di:], ((0, 0), (0, LANES - N_HEADS))).astype(BF16)
    bf = jnp.pad(b_f, (0, LANES - N_HEADS)).reshape(1, LANES)
    return pl.pallas_call(
        functools.partial(_forget_proj_kernel, per_batch=tokens_per_batch // tm),
        grid=(m // tm,),
        in_specs=[pl.BlockSpec((tm, d), lambda i: (i, 0)), _resident((1, d)), _resident(w.shape),
                  _resident(wvt.shape), _resident(wf.shape), _resident(bf.shape)],
        out_specs=[_head_block(tm, 2 * LANES), _head_block(tm, 2 * LANES),
                   pl.BlockSpec((N_HEADS, LANES, tm), lambda i: (0, 0, i)), _head_block(tm, LANES)],
        out_shape=[_head_major(m, 2 * LANES, BF16), _head_major(m, 2 * LANES, BF16),
                   jax.ShapeDtypeStruct((N_HEADS, LANES, m), BF16), _head_major(m, LANES, F32)],
        scratch_shapes=[pltpu.VMEM((8, LANES), F32)],
        compiler_params=_params("arbitrary"),
        name="forget_proj",
    )(x2d, g.reshape(1, d), w, wvt, wf, bf)


def _mla_proj_kernel(x_ref, pos_ref, inv_ref, g_ref, w1_ref, qn_ref, wq_ref, kvn_ref, wk_ref, wvt_ref,
                     qcat_ref, kcat_ref, vt_ref, gate_ref):
    h = _rmsnorm(x_ref[...], g_ref[...]).astype(BF16)
    lat = _dot(h, w1_ref[:, :5 * LANES])
    q_lat = lat[:, :MLA_Q_RANK]
    kv_lat = lat[:, MLA_Q_RANK:MLA_Q_RANK + MLA_KV_RANK]
    k_a = lat[:, 3 * LANES:4 * LANES]
    k_b = lat[:, 4 * LANES:5 * LANES]

    lane = lax.broadcasted_iota(jnp.int32, (1, LANES), 1)
    half = MLA_ROPE_DIM // 2
    ang = pos_ref[...] * inv_ref[...]
    cos = jnp.cos(ang)
    sin = jnp.where((lane // half) % 2 == 0, -1.0, 1.0) * jnp.sin(ang)
    k_rope = (k_a * cos + k_b * sin).astype(BF16)

    kvn = _rmsnorm(kv_lat, kvn_ref[...]).astype(BF16)
    step = 8
    for c0 in range(0, N_HEADS, step):
        k_nope = _dot(kvn, wk_ref[:, c0 * LANES:(c0 + step) * LANES])
        vt = _dot_nt(wvt_ref[c0 * LANES:(c0 + step) * LANES, :], kvn)
        for c in range(step):
            kcat_ref[c0 + c] = jnp.concatenate(
                [k_nope[:, c * LANES:(c + 1) * LANES].astype(BF16), k_rope], axis=1)
            vt_ref[c0 + c] = vt[c * LANES:(c + 1) * LANES, :].astype(BF16)

    qn = _rmsnorm(q_lat, qn_ref[...]).astype(BF16)
    n_pairs = N_HEADS // 2
    rope_a = _dot(qn, wq_ref[:, N_HEADS * LANES:(N_HEADS + n_pairs) * LANES])
    rope_b = _dot(qn, wq_ref[:, (N_HEADS + n_pairs) * LANES:])
    first_head = lane < MLA_ROPE_DIM
    q_scale = (MLA_NOPE_DIM + MLA_ROPE_DIM) ** -0.5 * LOG2E
    for c0 in range(0, N_HEADS, step):
        nope = _dot(qn, wq_ref[:, c0 * LANES:(c0 + step) * LANES])
        for c in range(step):
            hh = c0 + c
            pair = hh // 2
            sl = slice(pair * LANES, (pair + 1) * LANES)
            q_rope = rope_a[:, sl] * cos + rope_b[:, sl] * sin
            mine = first_head if hh % 2 == 0 else jnp.logical_not(first_head)
            q_rope = jnp.where(mine, q_rope * q_scale, 0.0)
            q_nope = nope[:, c * LANES:(c + 1) * LANES] * q_scale
            qcat_ref[hh] = jnp.concatenate([q_nope.astype(BF16), q_rope.astype(BF16)], axis=1)

    for c0 in range(0, N_HEADS, step):
        gt = _dot(h, w1_ref[:, (5 + c0) * LANES:(5 + c0 + step) * LANES])
        for c in range(step):
            gate_ref[c0 + c] = _silu(gt[:, c * LANES:(c + 1) * LANES])


def _mla_weights(w_in, w_qb, w_kvb):
    i1 = MLA_Q_RANK
    i2 = i1 + MLA_KV_RANK
    i3 = i2 + MLA_ROPE_DIM
    half = MLA_ROPE_DIM // 2
    wk1, wk2 = w_in[:, i2:i2 + half], w_in[:, i2 + half:i3]
    w1 = jnp.concatenate([w_in[:, :i2], wk1, wk2, wk1, wk2, wk2, wk1, wk2, wk1, w_in[:, i3:]], axis=1)
    wq = w_qb.reshape(MLA_Q_RANK, N_HEADS, MLA_NOPE_DIM + MLA_ROPE_DIM)
    nope = wq[:, :, :MLA_NOPE_DIM].reshape(MLA_Q_RANK, -1)
    r1 = wq[:, :, MLA_NOPE_DIM:MLA_NOPE_DIM + half]
    r2 = wq[:, :, MLA_NOPE_DIM + half:]
    rope_a = jnp.concatenate([r1, r2], axis=2).reshape(MLA_Q_RANK, -1)
    rope_b = jnp.concatenate([r2, r1], axis=2).reshape(MLA_Q_RANK, -1)
    wq2 = jnp.concatenate([nope, rope_a, rope_b], axis=1)
    wkv = w_kvb.reshape(MLA_KV_RANK, N_HEADS, 2 * LANES)
    wk = wkv[:, :, :MLA_NOPE_DIM].reshape(MLA_KV_RANK, -1)
    wvt = wkv[:, :, MLA_NOPE_DIM:].reshape(MLA_KV_RANK, -1).T
    return w1.astype(BF16), wq2.astype(BF16), wk.astype(BF16), wvt.astype(BF16)


def _mla_proj(x2d, positions, ln, w_in, q_norm, w_qb, kv_norm, w_kvb, tokens_per_batch):
    m, d = x2d.shape
    tm = min(256, tokens_per_batch)
    w1, wq2, wk, wvt = _mla_weights(w_in, w_qb, w_kvb)
    half = MLA_ROPE_DIM // 2
    inv_freq = ROPE_BASE ** (-jnp.arange(0, MLA_ROPE_DIM, 2, dtype=F32) / MLA_ROPE_DIM)
    inv4 = jnp.tile(inv_freq, LANES // half).reshape(1, LANES)
    pos = positions.astype(F32).reshape(m, 1)
    return pl.pallas_call(
        _mla_proj_kernel,
        grid=(m // tm,),
        in_specs=[pl.BlockSpec((tm, d), lambda i: (i, 0)),
                  pl.BlockSpec((tm, 1), lambda i: (i, 0)),
                  _resident((1, LANES)), _resident((1, d)), _resident(w1.shape),
                  _resident((1, MLA_Q_RANK)), _resident(wq2.shape),
                  _resident((1, MLA_KV_RANK)), _resident(wk.shape), _resident(wvt.shape)],
        out_specs=[_head_block(tm, 2 * LANES), _head_block(tm, 2 * LANES),
                   pl.BlockSpec((N_HEADS, LANES, tm), lambda i: (0, 0, i)), _head_block(tm, LANES)],
        out_shape=[_head_major(m, 2 * LANES, BF16), _head_major(m, 2 * LANES, BF16),
                   jax.ShapeDtypeStruct((N_HEADS, LANES, m), BF16), _head_major(m, LANES, F32)],
        compiler_params=_params("parallel"),
        name="mla_proj",
    )(x2d, pos, inv4, ln.reshape(1, d), w1, q_norm.reshape(1, -1), wq2, kv_norm.reshape(1, -1), wk, wvt)


def _out_kernel(o_ref, w_ref, x_ref, *rest, final):
    if final:
        g_ref, y_ref = rest
    else:
        (y_ref,) = rest
    o = jnp.concatenate([o_ref[hh] for hh in range(N_HEADS)], axis=1)
    y = x_ref[...] + _dot(o, w_ref[...])
    if final:
        y = _rmsnorm(y, g_ref[...])
    y_ref[...] = y


def _out_proj(o, w_out, x2d, final_g=None):
    m, d = x2d.shape
    tm = min(512, m)
    final = final_g is not None
    in_specs = [_head_block(tm, LANES), _resident((D_INNER, d)), pl.BlockSpec((tm, d), lambda i: (i, 0))]
    args = [o, w_out, x2d]
    if final:
        in_specs.append(_resident((1, d)))
        args.append(final_g.reshape(1, d))
    return pl.pallas_call(
        functools.partial(_out_kernel, final=final),
        grid=(m // tm,),
        in_specs=in_specs,
        out_specs=pl.BlockSpec((tm, d), lambda i: (i, 0)),
        out_shape=jax.ShapeDtypeStruct((m, d), F32),
        compiler_params=_params("parallel"),
        name="out_proj_final" if final else "out_proj",
    )(*args)


def _stick_kernel(q_ref, k_ref, v_ref, g_ref, suffix_ref, o_ref, acc_ref, c_ref, *, t):
    i = pl.program_id(2)
    n_heads = q_ref.shape[0]
    row = lax.broadcasted_iota(jnp.int32, (t, t), 0)
    col = lax.broadcasted_iota(jnp.int32, (t, t), 1)
    strict = col < row

    def keys(kj):
        return pl.ds(pl.multiple_of(kj * t, t), t)

    def skip_bits(z):
        return jnp.maximum(z, 0.0) + jnp.log2(1.0 + jnp.exp2(-jnp.abs(z)))

    def suffix_sums(u):
        return _dot(jnp.concatenate(_bf16_terms(u, 2), axis=1), suffix_ref[...])

    def row_sum(u):
        return jnp.sum(u, axis=1, keepdims=True)

    prev = jnp.maximum(i - 1, 0)
    heads = range(n_heads)
    z_diag, z_prev, u_diag, u_prev = [], [], [], []
    for hh in heads:
        z_diag.append(_dot_nt(q_ref[hh], k_ref[hh, keys(i), :]))
        z_prev.append(_dot_nt(q_ref[hh], k_ref[hh, keys(prev), :]))
    for hh in heads:
        u_diag.append(jnp.where(strict, skip_bits(z_diag[hh]), 0.0))
        u_prev.append(skip_bits(z_prev[hh]))
    later_diag = [suffix_sums(u_diag[hh]) for hh in heads]
    later_prev = [suffix_sums(u_prev[hh]) for hh in heads]
    state = []
    for hh in heads:
        w = jnp.where(strict, jnp.exp2(z_diag[hh] - later_diag[hh]), 0.0)
        acc = _dot(w.astype(BF16), v_ref[hh, keys(i), :])
        c = -row_sum(u_diag[hh])
        w = jnp.exp2(z_prev[hh] - later_prev[hh] + c)
        vb = v_ref[hh, keys(prev), :]
        acc = acc + _dot(w.astype(BF16), jnp.where(i > 0, vb, jnp.zeros_like(vb)))
        c = c - row_sum(u_prev[hh])
        acc_ref[hh] = acc
        c_ref[hh] = c
        state.append(jnp.max(c))
    c_max = functools.reduce(jnp.maximum, state)

    def more(carry):
        kj, c_max = carry
        return jnp.logical_and(kj >= 0, c_max > -UNDERFLOW_LOG2)

    def body(carry):
        kj, _ = carry
        c_maxes = []
        for hh in heads:
            z = _dot_nt(q_ref[hh], k_ref[hh, keys(kj), :])
            u = skip_bits(z)
            w = jnp.exp2(z - suffix_sums(u) + c_ref[hh])
            acc_ref[hh] += _dot(w.astype(BF16), v_ref[hh, keys(kj), :])
            c = c_ref[hh] - row_sum(u)
            c_ref[hh] = c
            c_maxes.append(jnp.max(c))
        return kj - 1, functools.reduce(jnp.maximum, c_maxes)

    lax.while_loop(more, body, (i - 2, c_max))
    for hh in heads:
        o_ref[hh] = (acc_ref[hh] * g_ref[hh]).astype(BF16)


def _stick_attention(qkv, gate, batch, seq):
    t = min(256, seq)
    nq = seq // t
    m = batch * seq
    h = N_HEADS
    hp = STICK_HEADS_PER_STEP
    groups = h // hp
    from_here = jnp.tril(jnp.ones((t, t), BF16))
    suffix_sum = jnp.concatenate([from_here, from_here], axis=0)
    return pl.pallas_call(
        functools.partial(_stick_kernel, t=t),
        grid=(groups, batch, nq),
        in_specs=[
            pl.BlockSpec((hp, t, LANES), lambda hh, b, i: (hh, b * nq + i, 0)),
            pl.BlockSpec((hp, seq, LANES), lambda hh, b, i: (groups + hh, b, 0)),
            pl.BlockSpec((hp, seq, LANES), lambda hh, b, i: (2 * groups + hh, b, 0)),
            pl.BlockSpec((hp, t, LANES), lambda hh, b, i: (hh, b * nq + i, 0)),
            _resident(suffix_sum.shape),
        ],
        out_specs=pl.BlockSpec((hp, t, LANES), lambda hh, b, i: (hh, b * nq + i, 0)),
        out_shape=jax.ShapeDtypeStruct((h, m, LANES), BF16),
        scratch_shapes=[pltpu.VMEM((hp, t, HEAD_DIM), F32), pltpu.VMEM((hp, t, 1), F32)],
        compiler_params=_params("parallel", "parallel", "arbitrary"),
        name="stick_attention",
    )(qkv, qkv, qkv, gate, suffix_sum)


def _softmax_kernel(q_ref, k_ref, vt_ref, g_ref, o_ref, s_ref, p_ref, *, t, chunked):
    n_heads, seq, _ = q_ref.shape
    heads = range(n_heads)
    key = lax.broadcasted_iota(jnp.int32, (t, t), 0)
    qry = lax.broadcasted_iota(jnp.int32, (t, t), 1)
    allowed = (key // CHUNK) <= (qry // CHUNK) if chunked else key <= qry

    subs = t // KEY_SUB

    def key_slice(g):
        if isinstance(g, int):
            return pl.ds(max(g, 0) * KEY_SUB, KEY_SUB)
        return pl.ds(pl.multiple_of(jnp.maximum(g, 0) * KEY_SUB, KEY_SUB), KEY_SUB)

    def softmax_step(hh, slot, mask, pv_prev, m_run, l_run, alpha_prev, acc):
        s = s_ref[hh, slot]
        s_max = jnp.max(s if mask is None else jnp.where(mask, s, NEG), axis=0, keepdims=True)
        m_new = jnp.maximum(m_run, s_max)
        alpha = jnp.exp2(m_run - m_new)
        p = jnp.exp2(s_ref[hh, slot] - m_new)
        if mask is not None:
            p = jnp.where(mask, p, 0.0)
        p_ref[hh, slot] = p.astype(BF16)
        l_run = alpha * l_run + jnp.sum(p, axis=0, keepdims=True)
        if pv_prev is not None:
            acc = alpha_prev * acc + pv_prev
        return m_new, l_run, alpha, acc

    for qi in range(seq // t):
        rows = slice(qi * t, (qi + 1) * t)
        top = (qi + 1) * subs - 1

        def scores(hh, g, slot):
            s_ref[hh, slot] = _dot_nt(k_ref[hh, key_slice(g), :], q_ref[hh, rows, :])

        def weighted_values(hh, g, slot):
            return _dot(vt_ref[hh, :, key_slice(g)], p_ref[hh, slot])

        def tile(first, states, diagonal):
            for j in range(subs):
                g, slot = first - j, j % 2
                mask = None
                if diagonal:
                    local = (subs - 1 - j) * KEY_SUB
                    mask = allowed[local:local + KEY_SUB]
                for hh in heads:
                    scores(hh, g - 1, 1 - slot)
                if diagonal and j == 0:
                    pv_prev = [None] * n_heads
                else:
                    pv_prev = [weighted_values(hh, g + 1, 1 - slot) for hh in heads]
                states = tuple(softmax_step(hh, slot, mask, pv_prev[hh], *states[hh]) for hh in heads)
            return states

        for hh in heads:
            scores(hh, top, 0)

        init = (jnp.full((1, t), NEG, F32), jnp.zeros((1, t), F32), jnp.ones((1, t), F32),
                jnp.zeros((HEAD_DIM, t), F32))
        states = tile(top, (init,) * n_heads, True)
        if qi:
            states = lax.fori_loop(0, qi, lambda n, st: tile(top - subs * (n + 1), st, False), states)
        for hh in heads:
            _, l_run, alpha, acc = states[hh]
            acc = alpha * acc + weighted_values(hh, 0, 1)
            o_ref[hh, rows, :] = ((acc * (1.0 / l_run)).T * g_ref[hh, rows, :]).astype(BF16)


def _softmax_attention(qcat, kcat, vt, gate, batch, seq, chunked):
    t = min(1024, seq)
    m = batch * seq
    h = N_HEADS
    width = qcat.shape[-1]
    hp = HEADS_PER_STEP
    tokens = lambda w: pl.BlockSpec((hp, seq, w), lambda hh, b: (hh, b, 0))
    return pl.pallas_call(
        functools.partial(_softmax_kernel, t=t, chunked=chunked),
        grid=(h // hp, batch),
        in_specs=[tokens(width), tokens(width),
                  pl.BlockSpec((hp, LANES, seq), lambda hh, b: (hh, 0, b)), tokens(LANES)],
        out_specs=tokens(LANES),
        scratch_shapes=[pltpu.VMEM((hp, 2, min(KEY_SUB, t), t), F32),
                        pltpu.VMEM((hp, 2, min(KEY_SUB, t), t), BF16)],
        out_shape=jax.ShapeDtypeStruct((h, m, LANES), BF16),
        compiler_params=_params("parallel", "parallel"),
        name="mla_attention" if chunked else "forget_attention",
    )(qcat, kcat, vt, gate)


def kernel(x, positions, ln0, w_in0, w_out0, ln1, w_in1, q_norm1, w_qb1, kv_norm1, w_kvb1, w_out1,
           ln2, w_in2, b_f2, w_out2, ln3, w_in3, w_out3, final_norm):
    batch, seq, d = x.shape
    x2d = x.reshape(batch * seq, d)

    qkv, gate = _norm_proj(x2d, ln0, w_in0.astype(BF16), seq)
    o = _stick_attention(qkv, gate, batch, seq)
    x2d = _out_proj(o, w_out0.astype(BF16), x2d)

    qcat, kcat, vt, gate = _mla_proj(x2d, positions, ln1, w_in1, q_norm1, w_qb1, kv_norm1, w_kvb1, seq)
    o = _softmax_attention(qcat, kcat, vt, gate, batch, seq, chunked=True)
    x2d = _out_proj(o, w_out1.astype(BF16), x2d)

    qcat, kcat, vt, gate = _forget_proj(x2d, ln2, w_in2, b_f2, seq)
    o = _softmax_attention(qcat, kcat, vt, gate, batch, seq, chunked=False)
    x2d = _out_proj(o, w_out2.astype(BF16), x2d)

    qkv, gate = _norm_proj(x2d, ln3, w_in3.astype(BF16), seq)
    o = _stick_attention(qkv, gate, batch, seq)
    x2d = _out_proj(o, w_out3.astype(BF16), x2d, final_g=final_norm)
    return x2d.reshape(batch, seq, d)
```

```python
import functools

import jax
import jax.numpy as jnp
from jax import lax
from jax.experimental import pallas as pl
from jax.experimental.pallas import tpu as pltpu

D_MODEL = 1024
N_HEADS = 16
HEAD_DIM = 128
D_INNER = N_HEADS * HEAD_DIM
CHUNK = 64
MLA_Q_RANK = 256
MLA_KV_RANK = 128
MLA_NOPE_DIM = 128
MLA_ROPE_DIM = 64
ROPE_BASE = 10000.0
EPS = 1e-6
NEG = -1e30
LOG2E = 1.4426950408889634
UNDERFLOW_LOG2 = 150.0

HEADS_PER_STEP = 2
STICK_HEADS_PER_STEP = 8
KEY_SUB = 512
LANES = 128
VMEM_LIMIT_BYTES = 56 * 1024 * 1024

F32 = jnp.float32
BF16 = jnp.bfloat16


def _params(*semantics):
    return pltpu.CompilerParams(dimension_semantics=semantics,
                                vmem_limit_bytes=VMEM_LIMIT_BYTES)


def _resident(shape):
    zeros = (0,) * len(shape)
    return pl.BlockSpec(shape, lambda *_: zeros, pipeline_mode=pl.Buffered(1))


def _rmsnorm(x, g):
    return x * lax.rsqrt(jnp.mean(x * x, axis=-1, keepdims=True) + EPS) * g


def _dot(a, b):
    return jnp.dot(a, b, preferred_element_type=F32)


def _dot_nt(a, b):
    return lax.dot_general(a, b, (((1,), (1,)), ((), ())), preferred_element_type=F32)


def _bf16_terms(a, terms):
    out = []
    for t in range(terms):
        part = a.astype(BF16)
        out.append(part)
        if t + 1 < terms:
            a = a - part.astype(F32)
    return out


def _silu(g):
    return g * (1.0 / (1.0 + jnp.exp(-g)))


def _head_major(m, width, dtype):
    return jax.ShapeDtypeStruct((N_HEADS, m, width), dtype)


def _head_block(tm, width):
    return pl.BlockSpec((N_HEADS, tm, width), lambda i: (0, i, 0))


def _proj_kernel(x_ref, g_ref, w_ref, qkv_ref, gate_ref):
    h = _rmsnorm(x_ref[...], g_ref[...]).astype(BF16)
    n_bf16 = 3 * N_HEADS
    step = 8
    for c0 in range(0, 4 * N_HEADS, step):
        acc = _dot(h, w_ref[:, c0 * LANES:(c0 + step) * LANES])
        for c in range(step):
            piece = acc[:, c * LANES:(c + 1) * LANES]
            if c0 + c < N_HEADS:
                qkv_ref[c0 + c] = (piece * (HEAD_DIM ** -0.5 * LOG2E)).astype(BF16)
            elif c0 + c < n_bf16:
                qkv_ref[c0 + c] = piece.astype(BF16)
            else:
                gate_ref[c0 + c - n_bf16] = _silu(piece)


def _norm_proj(x2d, g, w, tokens_per_batch):
    m, d = x2d.shape
    tm = min(256, tokens_per_batch)
    n_bf16 = 3 * N_HEADS
    return pl.pallas_call(
        _proj_kernel,
        grid=(m // tm,),
        in_specs=[pl.BlockSpec((tm, d), lambda i: (i, 0)), _resident((1, d)), _resident(w.shape)],
        out_specs=[pl.BlockSpec((n_bf16, tm, LANES), lambda i: (0, i, 0)), _head_block(tm, LANES)],
        out_shape=[jax.ShapeDtypeStruct((n_bf16, m, LANES), BF16), _head_major(m, LANES, F32)],
        compiler_params=_params("parallel"),
        name="norm_proj",
    )(x2d, g.reshape(1, d), w)


def _forget_proj_kernel(x_ref, g_ref, w_ref, wvt_ref, wf_ref, bf_ref,
                        qcat_ref, kcat_ref, vt_ref, gate_ref, carry_ref, *, per_batch):
    i = pl.program_id(0)
    tm = x_ref.shape[0]
    h = _rmsnorm(x_ref[...], g_ref[...]).astype(BF16)

    @pl.when(i % per_batch == 0)
    def _():
        carry_ref[...] = jnp.zeros_like(carry_ref)

    f = _dot(h, wf_ref[...]) + bf_ref[...]
    log_f = (jnp.minimum(f, 0.0) - jnp.log1p(jnp.exp(-jnp.abs(f)))) * LOG2E
    row = lax.broadcasted_iota(jnp.int32, (tm, tm), 0)
    col = lax.broadcasted_iota(jnp.int32, (tm, tm), 1)
    up_to = (col <= row).astype(BF16)
    cf = carry_ref[0:1, :]
    for part in _bf16_terms(log_f, 3):
        cf = cf + _dot(up_to, part)
    carry_ref[...] = jnp.broadcast_to(cf[tm - 1:tm, :], carry_ref.shape)

    hi, mid, lo = (part.astype(F32) for part in _bf16_terms(cf, 3))
    lane = lax.broadcasted_iota(jnp.int32, (1, LANES), 1)
    for hh in range(N_HEADS):
        terms = jnp.where(lane % 3 == 0, hi[:, hh:hh + 1],
                          jnp.where(lane % 3 == 1, mid[:, hh:hh + 1], lo[:, hh:hh + 1]))
        q_ext = jnp.where(lane < 3, terms, jnp.where(lane < 6, 1.0, 0.0))
        k_ext = jnp.where(lane < 3, 1.0, jnp.where(lane < 6, -terms, 0.0))
        qcat_ref[hh, :, LANES:] = q_ext.astype(BF16)
        kcat_ref[hh, :, LANES:] = k_ext.astype(BF16)

    step = 8
    for c0 in range(0, 3 * N_HEADS, step):
        acc = _dot(h, w_ref[:, c0 * LANES:(c0 + step) * LANES])
        for c in range(step):
            piece = acc[:, c * LANES:(c + 1) * LANES]
            g_idx = c0 + c
            if g_idx < N_HEADS:
                qcat_ref[g_idx, :, :LANES] = (piece * (HEAD_DIM ** -0.5 * LOG2E)).astype(BF16)
            elif g_idx < 2 * N_HEADS:
                kcat_ref[g_idx - N_HEADS, :, :LANES] = piece.astype(BF16)
            else:
                gate_ref[g_idx - 2 * N_HEADS] = _silu(piece)
    for c0 in range(0, N_HEADS, step):
        vt = _dot_nt(wvt_ref[c0 * LANES:(c0 + step) * LANES, :], h)
        for c in range(step):
            vt_ref[c0 + c] = vt[c * LANES:(c + 1) * LANES, :].astype(BF16)


def _forget_proj(x2d, g, w_in, b_f, tokens_per_batch):
    m, d = x2d.shape
    tm = min(256, tokens_per_batch)
    di = D_INNER
    w = jnp.concatenate([w_in[:, :2 * di], w_in[:, 3 * di:4 * di]], axis=1).astype(BF16)
    wvt = w_in[:, 2 * di:3 * di].T.astype(BF16)
    wf = jnp.pad(w_in[:, 4 * di:], ((0, 0), (0, LANES - N_HEADS))).astype(BF16)
    bf = jnp.pad(b_f, (0, LANES - N_HEADS)).reshape(1, LANES)
    return pl.pallas_call(
        functools.partial(_forget_proj_kernel, per_batch=tokens_per_batch // tm),
        grid=(m // tm,),
        in_specs=[pl.BlockSpec((tm, d), lambda i: (i, 0)), _resident((1, d)), _resident(w.shape),
                  _resident(wvt.shape), _resident(wf.shape), _resident(bf.shape)],
        out_specs=[_head_block(tm, 2 * LANES), _head_block(tm, 2 * LANES),
                   pl.BlockSpec((N_HEADS, LANES, tm), lambda i: (0, 0, i)), _head_block(tm, LANES)],
        out_shape=[_head_major(m, 2 * LANES, BF16), _head_major(m, 2 * LANES, BF16),
                   jax.ShapeDtypeStruct((N_HEADS, LANES, m), BF16), _head_major(m, LANES, F32)],
        scratch_shapes=[pltpu.VMEM((8, LANES), F32)],
        compiler_params=_params("arbitrary"),
        name="forget_proj",
    )(x2d, g.reshape(1, d), w, wvt, wf, bf)


def _mla_proj_kernel(x_ref, pos_ref, inv_ref, g_ref, w1_ref, qn_ref, wq_ref, kvn_ref, wk_ref, wvt_ref,
                     qcat_ref, kcat_ref, vt_ref, gate_ref):
    h = _rmsnorm(x_ref[...], g_ref[...]).astype(BF16)
    lat = _dot(h, w1_ref[:, :5 * LANES])
    q_lat = lat[:, :MLA_Q_RANK]
    kv_lat = lat[:, MLA_Q_RANK:MLA_Q_RANK + MLA_KV_RANK]
    k_a = lat[:, 3 * LANES:4 * LANES]
    k_b = lat[:, 4 * LANES:5 * LANES]

    lane = lax.broadcasted_iota(jnp.int32, (1, LANES), 1)
    half = MLA_ROPE_DIM // 2
    ang = pos_ref[...] * inv_ref[...]
    cos = jnp.cos(ang)
    sin = jnp.where((lane // half) % 2 == 0, -1.0, 1.0) * jnp.sin(ang)
    k_rope = (k_a * cos + k_b * sin).astype(BF16)

    kvn = _rmsnorm(kv_lat, kvn_ref[...]).astype(BF16)
    step = 8
    for c0 in range(0, N_HEADS, step):
        k_nope = _dot(kvn, wk_ref[:, c0 * LANES:(c0 + step) * LANES])
        vt = _dot_nt(wvt_ref[c0 * LANES:(c0 + step) * LANES, :], kvn)
        for c in range(step):
            kcat_ref[c0 + c] = jnp.concatenate(
                [k_nope[:, c * LANES:(c + 1) * LANES].astype(BF16), k_rope], axis=1)
            vt_ref[c0 + c] = vt[c * LANES:(c + 1) * LANES, :].astype(BF16)

    qn = _rmsnorm(q_lat, qn_ref[...]).astype(BF16)
    n_pairs = N_HEADS // 2
    rope_a = _dot(qn, wq_ref[:, N_HEADS * LANES:(N_HEADS + n_pairs) * LANES])
    rope_b = _dot(qn, wq_ref[:, (N_HEADS + n_pairs) * LANES:])
    first_head = lane < MLA_ROPE_DIM
    q_scale = (MLA_NOPE_DIM + MLA_ROPE_DIM) ** -0.5 * LOG2E
    for c0 in range(0, N_HEADS, step):
        nope = _dot(qn, wq_ref[:, c0 * LANES:(c0 + step) * LANES])
        for c in range(step):
            hh = c0 + c
            pair = hh // 2
            sl = slice(pair * LANES, (pair + 1) * LANES)
            q_rope = rope_a[:, sl] * cos + rope_b[:, sl] * sin
            mine = first_head if hh % 2 == 0 else jnp.logical_not(first_head)
            q_rope = jnp.where(mine, q_rope * q_scale, 0.0)
            q_nope = nope[:, c * LANES:(c + 1) * LANES] * q_scale
            qcat_ref[hh] = jnp.concatenate([q_nope.astype(BF16), q_rope.astype(BF16)], axis=1)

    for c0 in range(0, N_HEADS, step):
        gt = _dot(h, w1_ref[:, (5 + c0) * LANES:(5 + c0 + step) * LANES])
        for c in range(step):
            gate_ref[c0 + c] = _silu(gt[:, c * LANES:(c + 1) * LANES])


def _mla_weights(w_in, w_qb, w_kvb):
    i1 = MLA_Q_RANK
    i2 = i1 + MLA_KV_RANK
    i3 = i2 + MLA_ROPE_DIM
    half = MLA_ROPE_DIM // 2
    wk1, wk2 = w_in[:, i2:i2 + half], w_in[:, i2 + half:i3]
    w1 = jnp.concatenate([w_in[:, :i2], wk1, wk2, wk1, wk2, wk2, wk1, wk2, wk1, w_in[:, i3:]], axis=1)
    wq = w_qb.reshape(MLA_Q_RANK, N_HEADS, MLA_NOPE_DIM + MLA_ROPE_DIM)
    nope = wq[:, :, :MLA_NOPE_DIM].reshape(MLA_Q_RANK, -1)
    r1 = wq[:, :, MLA_NOPE_DIM:MLA_NOPE_DIM + half]
    r2 = wq[:, :, MLA_NOPE_DIM + half:]
    rope_a = jnp.concatenate([r1, r2], axis=2).reshape(MLA_Q_RANK, -1)
    rope_b = jnp.concatenate([r2, r1], axis=2).reshape(MLA_Q_RANK, -1)
    wq2 = jnp.concatenate([nope, rope_a, rope_b], axis=1)
    wkv = w_kvb.reshape(MLA_KV_RANK, N_HEADS, 2 * LANES)
    wk = wkv[:, :, :MLA_NOPE_DIM].reshape(MLA_KV_RANK, -1)
    wvt = wkv[:, :, MLA_NOPE_DIM:].reshape(MLA_KV_RANK, -1).T
    return w1.astype(BF16), wq2.astype(BF16), wk.astype(BF16), wvt.astype(BF16)


def _mla_proj(x2d, positions, ln, w_in, q_norm, w_qb, kv_norm, w_kvb, tokens_per_batch):
    m, d = x2d.shape
    tm = min(256, tokens_per_batch)
    w1, wq2, wk, wvt = _mla_weights(w_in, w_qb, w_kvb)
    half = MLA_ROPE_DIM // 2
    inv_freq = ROPE_BASE ** (-jnp.arange(0, MLA_ROPE_DIM, 2, dtype=F32) / MLA_ROPE_DIM)
    inv4 = jnp.tile(inv_freq, LANES // half).reshape(1, LANES)
    pos = positions.astype(F32).reshape(m, 1)
    return pl.pallas_call(
        _mla_proj_kernel,
        grid=(m // tm,),
        in_specs=[pl.BlockSpec((tm, d), lambda i: (i, 0)),
                  pl.BlockSpec((tm, 1), lambda i: (i, 0)),
                  _resident((1, LANES)), _resident((1, d)), _resident(w1.shape),
                  _resident((1, MLA_Q_RANK)), _resident(wq2.shape),
                  _resident((1, MLA_KV_RANK)), _resident(wk.shape), _resident(wvt.shape)],
        out_specs=[_head_block(tm, 2 * LANES), _head_block(tm, 2 * LANES),
                   pl.BlockSpec((N_HEADS, LANES, tm), lambda i: (0, 0, i)), _head_block(tm, LANES)],
        out_shape=[_head_major(m, 2 * LANES, BF16), _head_major(m, 2 * LANES, BF16),
                   jax.ShapeDtypeStruct((N_HEADS, LANES, m), BF16), _head_major(m, LANES, F32)],
        compiler_params=_params("parallel"),
        name="mla_proj",
    )(x2d, pos, inv4, ln.reshape(1, d), w1, q_norm.reshape(1, -1), wq2, kv_norm.reshape(1, -1), wk, wvt)


def _out_kernel(o_ref, w_ref, x_ref, *rest, final):
    if final:
        g_ref, y_ref = rest
    else:
        (y_ref,) = rest
    o = jnp.concatenate([o_ref[hh] for hh in range(N_HEADS)], axis=1)
    y = x_ref[...] + _dot(o, w_ref[...])
    if final:
        y = _rmsnorm(y, g_ref[...])
    y_ref[...] = y


def _out_proj(o, w_out, x2d, final_g=None):
    m, d = x2d.shape
    tm = min(512, m)
    final = final_g is not None
    in_specs = [_head_block(tm, LANES), _resident((D_INNER, d)), pl.BlockSpec((tm, d), lambda i: (i, 0))]
    args = [o, w_out, x2d]
    if final:
        in_specs.append(_resident((1, d)))
        args.append(final_g.reshape(1, d))
    return pl.pallas_call(
        functools.partial(_out_kernel, final=final),
        grid=(m // tm,),
        in_specs=in_specs,
        out_specs=pl.BlockSpec((tm, d), lambda i: (i, 0)),
        out_shape=jax.ShapeDtypeStruct((m, d), F32),
        compiler_params=_params("parallel"),
        name="out_proj_final" if final else "out_proj",
    )(*args)


def _stick_kernel(q_ref, k_ref, v_ref, g_ref, suffix_ref, o_ref, acc_ref, c_ref, *, t):
    i = pl.program_id(2)
    n_heads = q_ref.shape[0]
    row = lax.broadcasted_iota(jnp.int32, (t, t), 0)
    col = lax.broadcasted_iota(jnp.int32, (t, t), 1)
    strict = col < row

    def keys(kj):
        return pl.ds(pl.multiple_of(kj * t, t), t)

    def skip_bits(z):
        return jnp.maximum(z, 0.0) + jnp.log2(1.0 + jnp.exp2(-jnp.abs(z)))

    def suffix_sums(u):
        return _dot(jnp.concatenate(_bf16_terms(u, 2), axis=1), suffix_ref[...])

    def row_sum(u):
        return jnp.sum(u, axis=1, keepdims=True)

    prev = jnp.maximum(i - 1, 0)
    heads = range(n_heads)
    z_diag, z_prev, u_diag, u_prev = [], [], [], []
    for hh in heads:
        z_diag.append(_dot_nt(q_ref[hh], k_ref[hh, keys(i), :]))
        z_prev.append(_dot_nt(q_ref[hh], k_ref[hh, keys(prev), :]))
    for hh in heads:
        u_diag.append(jnp.where(strict, skip_bits(z_diag[hh]), 0.0))
        u_prev.append(skip_bits(z_prev[hh]))
    later_diag = [suffix_sums(u_diag[hh]) for hh in heads]
    later_prev = [suffix_sums(u_prev[hh]) for hh in heads]
    state = []
    for hh in heads:
        w = jnp.where(strict, jnp.exp2(z_diag[hh] - later_diag[hh]), 0.0)
        acc = _dot(w.astype(BF16), v_ref[hh, keys(i), :])
        c = -row_sum(u_diag[hh])
        w = jnp.exp2(z_prev[hh] - later_prev[hh] + c)
        vb = v_ref[hh, keys(prev), :]
        acc = acc + _dot(w.astype(BF16), jnp.where(i > 0, vb, jnp.zeros_like(vb)))
        c = c - row_sum(u_prev[hh])
        acc_ref[hh] = acc
        c_ref[hh] = c
        state.append(jnp.max(c))
    c_max = functools.reduce(jnp.maximum, state)

    def more(carry):
        kj, c_max = carry
        return jnp.logical_and(kj >= 0, c_max > -UNDERFLOW_LOG2)

    def body(carry):
        kj, _ = carry
        c_maxes = []
        for hh in heads:
            z = _dot_nt(q_ref[hh], k_ref[hh, keys(kj), :])
            u = skip_bits(z)
            w = jnp.exp2(z - suffix_sums(u) + c_ref[hh])
            acc_ref[hh] += _dot(w.astype(BF16), v_ref[hh, keys(kj), :])
            c = c_ref[hh] - row_sum(u)
            c_ref[hh] = c
            c_maxes.append(jnp.max(c))
        return kj - 1, functools.reduce(jnp.maximum, c_maxes)

    lax.while_loop(more, body, (i - 2, c_max))
    for hh in heads:
        o_ref[hh] = (acc_ref[hh] * g_ref[hh]).astype(BF16)


def _stick_attention(qkv, gate, batch, seq):
    t = min(256, seq)
    nq = seq // t
    m = batch * seq
    h = N_HEADS
    hp = STICK_HEADS_PER_STEP
    groups = h // hp
    from_here = jnp.tril(jnp.ones((t, t), BF16))
    suffix_sum = jnp.concatenate([from_here, from_here], axis=0)
    return pl.pallas_call(
        functools.partial(_stick_kernel, t=t),
        grid=(groups, batch, nq),
        in_specs=[
            pl.BlockSpec((hp, t, LANES), lambda hh, b, i: (hh, b * nq + i, 0)),
            pl.BlockSpec((hp, seq, LANES), lambda hh, b, i: (groups + hh, b, 0)),
            pl.BlockSpec((hp, seq, LANES), lambda hh, b, i: (2 * groups + hh, b, 0)),
            pl.BlockSpec((hp, t, LANES), lambda hh, b, i: (hh, b * nq + i, 0)),
            _resident(suffix_sum.shape),
        ],
        out_specs=pl.BlockSpec((hp, t, LANES), lambda hh, b, i: (hh, b * nq + i, 0)),
        out_shape=jax.ShapeDtypeStruct((h, m, LANES), BF16),
        scratch_shapes=[pltpu.VMEM((hp, t, HEAD_DIM), F32), pltpu.VMEM((hp, t, 1), F32)],
        compiler_params=_params("parallel", "parallel", "arbitrary"),
        name="stick_attention",
    )(qkv, qkv, qkv, gate, suffix_sum)


def _softmax_kernel(q_ref, k_ref, vt_ref, g_ref, o_ref, s_ref, p_ref, *, t, chunked):
    n_heads, seq, _ = q_ref.shape
    heads = range(n_heads)
    key = lax.broadcasted_iota(jnp.int32, (KEY_SUB, KEY_SUB), 0)
    qry = lax.broadcasted_iota(jnp.int32, (KEY_SUB, KEY_SUB), 1)
    on_diagonal = (key // CHUNK) <= (qry // CHUNK) if chunked else key <= qry

    subs = t // KEY_SUB

    def key_slice(g):
        if isinstance(g, int):
            return pl.ds(max(g, 0) * KEY_SUB, KEY_SUB)
        return pl.ds(pl.multiple_of(jnp.maximum(g, 0) * KEY_SUB, KEY_SUB), KEY_SUB)

    def softmax_step(hh, slot, diag_chunk, pv_prev, m_run, l_run, alpha_prev, acc):
        if diag_chunk is None:
            m_new = jnp.maximum(m_run, jnp.max(s_ref[hh, slot], axis=0, keepdims=True))
            p = jnp.exp2(s_ref[hh, slot] - m_new)
        else:
            lanes = [slice(c * KEY_SUB, (c + 1) * KEY_SUB) for c in range(subs)]
            pieces = []
            for c in range(subs):
                if c < diag_chunk:
                    pieces.append(jnp.full((KEY_SUB, KEY_SUB), NEG, F32))
                elif c == diag_chunk:
                    pieces.append(jnp.where(on_diagonal, s_ref[hh, slot, :, lanes[c]], NEG))
                else:
                    pieces.append(s_ref[hh, slot, :, lanes[c]])
            s_max = jnp.max(jnp.concatenate(pieces, axis=1), axis=0, keepdims=True)
            m_new = jnp.maximum(m_run, s_max)
            p = jnp.concatenate(
                [jnp.zeros((KEY_SUB, KEY_SUB), F32) if c < diag_chunk
                 else jnp.exp2(pieces[c] - m_new[:, lanes[c]]) for c in range(subs)], axis=1)
        alpha = jnp.exp2(m_run - m_new)
        p_ref[hh, slot] = p.astype(BF16)
        l_run = alpha * l_run + jnp.sum(p, axis=0, keepdims=True)
        if pv_prev is not None:
            acc = alpha_prev * acc + pv_prev
        return m_new, l_run, alpha, acc

    n_tiles = seq // t

    def scores(hh, qi, g, slot):
        rows = slice(qi * t, (qi + 1) * t)
        s_ref[hh, slot] = _dot_nt(k_ref[hh, key_slice(g), :], q_ref[hh, rows, :])

    def weighted_values(hh, g, slot):
        return _dot(vt_ref[hh, :, key_slice(g)], p_ref[hh, slot])

    for hh in heads:
        scores(hh, 0, subs - 1, 0)

    for qi in range(n_tiles):
        rows = slice(qi * t, (qi + 1) * t)
        top = (qi + 1) * subs - 1

        def tile(first, states, diagonal, last):
            for j in range(subs):
                g, slot = first - j, j % 2
                diag_chunk = subs - 1 - j if diagonal else None
                for hh in heads:
                    if not (last and j == subs - 1):
                        scores(hh, qi, g - 1, 1 - slot)
                    elif qi + 1 < n_tiles:
                        scores(hh, qi + 1, top + subs, 1 - slot)
                if diagonal and j == 0:
                    pv_prev = [None] * n_heads
                else:
                    pv_prev = [weighted_values(hh, g + 1, 1 - slot) for hh in heads]
                states = tuple(softmax_step(hh, slot, diag_chunk, pv_prev[hh], *states[hh]) for hh in heads)
            return states

        init = (jnp.full((1, t), NEG, F32), jnp.zeros((1, t), F32), jnp.ones((1, t), F32),
                jnp.zeros((HEAD_DIM, t), F32))
        states = tile(top, (init,) * n_heads, True, qi == 0)
        if qi > 1:
            states = lax.fori_loop(
                0, qi - 1, lambda n, st: tile(top - subs * (n + 1), st, False, False), states)
        if qi > 0:
            states = tile(subs - 1, states, False, True)
        for hh in heads:
            _, l_run, alpha, acc = states[hh]
            acc = alpha * acc + weighted_values(hh, 0, 1)
            o_ref[hh, rows, :] = ((acc * (1.0 / l_run)).T * g_ref[hh, rows, :]).astype(BF16)


def _softmax_attention(qcat, kcat, vt, gate, batch, seq, chunked):
    t = min(1024, seq)
    m = batch * seq
    h = N_HEADS
    width = qcat.shape[-1]
    hp = HEADS_PER_STEP
    tokens = lambda w: pl.BlockSpec((hp, seq, w), lambda hh, b: (hh, b, 0))
    return pl.pallas_call(
        functools.partial(_softmax_kernel, t=t, chunked=chunked),
        grid=(h // hp, batch),
        in_specs=[tokens(width), tokens(width),
                  pl.BlockSpec((hp, LANES, seq), lambda hh, b: (hh, 0, b)), tokens(LANES)],
        out_specs=tokens(LANES),
        scratch_shapes=[pltpu.VMEM((hp, 2, min(KEY_SUB, t), t), F32),
                        pltpu.VMEM((hp, 2, min(KEY_SUB, t), t), BF16)],
        out_shape=jax.ShapeDtypeStruct((h, m, LANES), BF16),
        compiler_params=_params("parallel", "parallel"),
        name="mla_attention" if chunked else "forget_attention",
    )(qcat, kcat, vt, gate)


def kernel(x, positions, ln0, w_in0, w_out0, ln1, w_in1, q_norm1, w_qb1, kv_norm1, w_kvb1, w_out1,
           ln2, w_in2, b_f2, w_out2, ln3, w_in3, w_out3, final_norm):
    batch, seq, d = x.shape
    x2d = x.reshape(batch * seq, d)

    qkv, gate = _norm_proj(x2d, ln0, w_in0.astype(BF16), seq)
    o = _stick_attention(qkv, gate, batch, seq)
    x2d = _out_proj(o, w_out0.astype(BF16), x2d)

    qcat, kcat, vt, gate = _mla_proj(x2d, positions, ln1, w_in1, q_norm1, w_qb1, kv_norm1, w_kvb1, seq)
    o = _softmax_attention(qcat, kcat, vt, gate, batch, seq, chunked=True)
    x2d = _out_proj(o, w_out1.astype(BF16), x2d)

    qcat, kcat, vt, gate = _forget_proj(x2d, ln2, w_in2, b_f2, seq)
    o = _softmax_attention(qcat, kcat, vt, gate, batch, seq, chunked=False)
    x2d = _out_proj(o, w_out2.astype(BF16), x2d)

    qkv, gate = _norm_proj(x2d, ln3, w_in3.astype(BF16), seq)
    o = _stick_attention(qkv, gate, batch, seq)
    x2d = _out_proj(o, w_out3.astype(BF16), x2d, final_g=final_norm)
    return x2d.reshape(batch, seq, d)
```

```python
import functools

import jax
import jax.numpy as jnp
from jax import lax
from jax.experimental import pallas as pl
from jax.experimental.pallas import tpu as pltpu

N_HEADS = 16
HEAD_DIM = 128
D_INNER = N_HEADS * HEAD_DIM
CHUNK = 64
MLA_Q_RANK = 256
MLA_KV_RANK = 128
MLA_NOPE_DIM = 128
MLA_ROPE_DIM = 64
ROPE_BASE = 10000.0
EPS = 1e-6
NEG = -1e30
LOG2E = 1.4426950408889634
UNDERFLOW_LOG2 = 150.0

PROJ_ROWS = 256
OUT_ROWS = 512
STICK_BLOCK = 256
STICK_HEADS_PER_STEP = 8
SOFTMAX_TILE = 1024
KEY_SUB = 512
HEADS_PER_STEP = 2
LANES = 128
VMEM_LIMIT_BYTES = 56 * 1024 * 1024

F32 = jnp.float32
BF16 = jnp.bfloat16


def _params(*semantics):
    return pltpu.CompilerParams(dimension_semantics=semantics,
                                vmem_limit_bytes=VMEM_LIMIT_BYTES)


def _resident(shape):
    zeros = (0,) * len(shape)
    return pl.BlockSpec(shape, lambda *_: zeros, pipeline_mode=pl.Buffered(1))


def _rmsnorm(x, g):
    return x * lax.rsqrt(jnp.mean(x * x, axis=-1, keepdims=True) + EPS) * g


def _dot(a, b):
    return jnp.dot(a, b, preferred_element_type=F32)


def _dot_nt(a, b):
    return lax.dot_general(a, b, (((1,), (1,)), ((), ())), preferred_element_type=F32)


def _bf16_terms(a, terms):
    out = []
    for t in range(terms):
        part = a.astype(BF16)
        out.append(part)
        if t + 1 < terms:
            a = a - part.astype(F32)
    return out


def _silu(g):
    return g * (1.0 / (1.0 + jnp.exp(-g)))


def _head_major(m, width, dtype):
    return jax.ShapeDtypeStruct((N_HEADS, m, width), dtype)


def _head_block(tm, width):
    return pl.BlockSpec((N_HEADS, tm, width), lambda i: (0, i, 0))


def _proj_kernel(x_ref, g_ref, w_ref, qkv_ref, gate_ref):
    h = _rmsnorm(x_ref[...], g_ref[...]).astype(BF16)
    n_bf16 = 3 * N_HEADS
    step = 8
    for c0 in range(0, 4 * N_HEADS, step):
        acc = _dot(h, w_ref[:, c0 * LANES:(c0 + step) * LANES])
        for c in range(step):
            piece = acc[:, c * LANES:(c + 1) * LANES]
            if c0 + c < N_HEADS:
                qkv_ref[c0 + c] = (piece * (HEAD_DIM ** -0.5 * LOG2E)).astype(BF16)
            elif c0 + c < n_bf16:
                qkv_ref[c0 + c] = piece.astype(BF16)
            else:
                gate_ref[c0 + c - n_bf16] = _silu(piece)


def _norm_proj(x2d, g, w, tokens_per_batch):
    m, d = x2d.shape
    tm = min(PROJ_ROWS, tokens_per_batch)
    n_bf16 = 3 * N_HEADS
    return pl.pallas_call(
        _proj_kernel,
        grid=(m // tm,),
        in_specs=[pl.BlockSpec((tm, d), lambda i: (i, 0)), _resident((1, d)), _resident(w.shape)],
        out_specs=[pl.BlockSpec((n_bf16, tm, LANES), lambda i: (0, i, 0)), _head_block(tm, LANES)],
        out_shape=[jax.ShapeDtypeStruct((n_bf16, m, LANES), BF16), _head_major(m, LANES, F32)],
        compiler_params=_params("parallel"),
        name="norm_proj",
    )(x2d, g.reshape(1, d), w)


def _forget_proj_kernel(x_ref, g_ref, w_ref, wvt_ref, wf_ref, bf_ref,
                        qcat_ref, kcat_ref, vt_ref, gate_ref, carry_ref, *, per_batch):
    i = pl.program_id(0)
    tm = x_ref.shape[0]
    h = _rmsnorm(x_ref[...], g_ref[...]).astype(BF16)

    @pl.when(i % per_batch == 0)
    def _():
        carry_ref[...] = jnp.zeros_like(carry_ref)

    f = _dot(h, wf_ref[...]) + bf_ref[...]
    log_f = (jnp.minimum(f, 0.0) - jnp.log1p(jnp.exp(-jnp.abs(f)))) * LOG2E
    row = lax.broadcasted_iota(jnp.int32, (tm, tm), 0)
    col = lax.broadcasted_iota(jnp.int32, (tm, tm), 1)
    up_to = (col <= row).astype(BF16)
    cf = carry_ref[0:1, :]
    for part in _bf16_terms(log_f, 3):
        cf = cf + _dot(up_to, part)
    carry_ref[...] = jnp.broadcast_to(cf[tm - 1:tm, :], carry_ref.shape)

    hi, mid, lo = (part.astype(F32) for part in _bf16_terms(cf, 3))
    lane = lax.broadcasted_iota(jnp.int32, (1, LANES), 1)
    for hh in range(N_HEADS):
        terms = jnp.where(lane % 3 == 0, hi[:, hh:hh + 1],
                          jnp.where(lane % 3 == 1, mid[:, hh:hh + 1], lo[:, hh:hh + 1]))
        q_ext = jnp.where(lane < 3, terms, jnp.where(lane < 6, 1.0, 0.0))
        k_ext = jnp.where(lane < 3, 1.0, jnp.where(lane < 6, -terms, 0.0))
        qcat_ref[hh, :, LANES:] = q_ext.astype(BF16)
        kcat_ref[hh, :, LANES:] = k_ext.astype(BF16)

    step = 8
    for c0 in range(0, 3 * N_HEADS, step):
        acc = _dot(h, w_ref[:, c0 * LANES:(c0 + step) * LANES])
        for c in range(step):
            piece = acc[:, c * LANES:(c + 1) * LANES]
            g_idx = c0 + c
            if g_idx < N_HEADS:
                qcat_ref[g_idx, :, :LANES] = (piece * (HEAD_DIM ** -0.5 * LOG2E)).astype(BF16)
            elif g_idx < 2 * N_HEADS:
                kcat_ref[g_idx - N_HEADS, :, :LANES] = piece.astype(BF16)
            else:
                gate_ref[g_idx - 2 * N_HEADS] = _silu(piece)
    for c0 in range(0, N_HEADS, step):
        vt = _dot_nt(wvt_ref[c0 * LANES:(c0 + step) * LANES, :], h)
        for c in range(step):
            vt_ref[c0 + c] = vt[c * LANES:(c + 1) * LANES, :].astype(BF16)


def _forget_proj(x2d, g, w_in, b_f, tokens_per_batch):
    m, d = x2d.shape
    tm = min(PROJ_ROWS, tokens_per_batch)
    di = D_INNER
    w = jnp.concatenate([w_in[:, :2 * di], w_in[:, 3 * di:4 * di]], axis=1).astype(BF16)
    wvt = w_in[:, 2 * di:3 * di].T.astype(BF16)
    wf = jnp.pad(w_in[:, 4 * di:], ((0, 0), (0, LANES - N_HEADS))).astype(BF16)
    bf = jnp.pad(b_f, (0, LANES - N_HEADS)).reshape(1, LANES)
    return pl.pallas_call(
        functools.partial(_forget_proj_kernel, per_batch=tokens_per_batch // tm),
        grid=(m // tm,),
        in_specs=[pl.BlockSpec((tm, d), lambda i: (i, 0)), _resident((1, d)), _resident(w.shape),
                  _resident(wvt.shape), _resident(wf.shape), _resident(bf.shape)],
        out_specs=[_head_block(tm, 2 * LANES), _head_block(tm, 2 * LANES),
                   pl.BlockSpec((N_HEADS, LANES, tm), lambda i: (0, 0, i)), _head_block(tm, LANES)],
        out_shape=[_head_major(m, 2 * LANES, BF16), _head_major(m, 2 * LANES, BF16),
                   jax.ShapeDtypeStruct((N_HEADS, LANES, m), BF16), _head_major(m, LANES, F32)],
        scratch_shapes=[pltpu.VMEM((8, LANES), F32)],
        compiler_params=_params("arbitrary"),
        name="forget_proj",
    )(x2d, g.reshape(1, d), w, wvt, wf, bf)


def _mla_proj_kernel(x_ref, pos_ref, inv_ref, g_ref, w1_ref, qn_ref, wq_ref, kvn_ref, wk_ref, wvt_ref,
                     qcat_ref, kcat_ref, vt_ref, gate_ref):
    h = _rmsnorm(x_ref[...], g_ref[...]).astype(BF16)
    lat = _dot(h, w1_ref[:, :5 * LANES])
    q_lat = lat[:, :MLA_Q_RANK]
    kv_lat = lat[:, MLA_Q_RANK:MLA_Q_RANK + MLA_KV_RANK]
    k_a = lat[:, 3 * LANES:4 * LANES]
    k_b = lat[:, 4 * LANES:5 * LANES]

    lane = lax.broadcasted_iota(jnp.int32, (1, LANES), 1)
    half = MLA_ROPE_DIM // 2
    ang = pos_ref[...] * inv_ref[...]
    cos = jnp.cos(ang)
    sin = jnp.where((lane // half) % 2 == 0, -1.0, 1.0) * jnp.sin(ang)
    k_rope = (k_a * cos + k_b * sin).astype(BF16)

    kvn = _rmsnorm(kv_lat, kvn_ref[...]).astype(BF16)
    step = 8
    for c0 in range(0, N_HEADS, step):
        k_nope = _dot(kvn, wk_ref[:, c0 * LANES:(c0 + step) * LANES])
        vt = _dot_nt(wvt_ref[c0 * LANES:(c0 + step) * LANES, :], kvn)
        for c in range(step):
            kcat_ref[c0 + c] = jnp.concatenate(
                [k_nope[:, c * LANES:(c + 1) * LANES].astype(BF16), k_rope], axis=1)
            vt_ref[c0 + c] = vt[c * LANES:(c + 1) * LANES, :].astype(BF16)

    qn = _rmsnorm(q_lat, qn_ref[...]).astype(BF16)
    n_pairs = N_HEADS // 2
    rope_a = _dot(qn, wq_ref[:, N_HEADS * LANES:(N_HEADS + n_pairs) * LANES])
    rope_b = _dot(qn, wq_ref[:, (N_HEADS + n_pairs) * LANES:])
    first_head = lane < MLA_ROPE_DIM
    q_scale = (MLA_NOPE_DIM + MLA_ROPE_DIM) ** -0.5 * LOG2E
    for c0 in range(0, N_HEADS, step):
        nope = _dot(qn, wq_ref[:, c0 * LANES:(c0 + step) * LANES])
        for c in range(step):
            hh = c0 + c
            pair = hh // 2
            sl = slice(pair * LANES, (pair + 1) * LANES)
            q_rope = rope_a[:, sl] * cos + rope_b[:, sl] * sin
            mine = first_head if hh % 2 == 0 else jnp.logical_not(first_head)
            q_rope = jnp.where(mine, q_rope * q_scale, 0.0)
            q_nope = nope[:, c * LANES:(c + 1) * LANES] * q_scale
            qcat_ref[hh] = jnp.concatenate([q_nope.astype(BF16), q_rope.astype(BF16)], axis=1)

    for c0 in range(0, N_HEADS, step):
        gt = _dot(h, w1_ref[:, (5 + c0) * LANES:(5 + c0 + step) * LANES])
        for c in range(step):
            gate_ref[c0 + c] = _silu(gt[:, c * LANES:(c + 1) * LANES])


def _mla_weights(w_in, w_qb, w_kvb):
    i1 = MLA_Q_RANK
    i2 = i1 + MLA_KV_RANK
    i3 = i2 + MLA_ROPE_DIM
    half = MLA_ROPE_DIM // 2
    wk1, wk2 = w_in[:, i2:i2 + half], w_in[:, i2 + half:i3]
    w1 = jnp.concatenate([w_in[:, :i2], wk1, wk2, wk1, wk2, wk2, wk1, wk2, wk1, w_in[:, i3:]], axis=1)
    wq = w_qb.reshape(MLA_Q_RANK, N_HEADS, MLA_NOPE_DIM + MLA_ROPE_DIM)
    nope = wq[:, :, :MLA_NOPE_DIM].reshape(MLA_Q_RANK, -1)
    r1 = wq[:, :, MLA_NOPE_DIM:MLA_NOPE_DIM + half]
    r2 = wq[:, :, MLA_NOPE_DIM + half:]
    rope_a = jnp.concatenate([r1, r2], axis=2).reshape(MLA_Q_RANK, -1)
    rope_b = jnp.concatenate([r2, r1], axis=2).reshape(MLA_Q_RANK, -1)
    wq2 = jnp.concatenate([nope, rope_a, rope_b], axis=1)
    wkv = w_kvb.reshape(MLA_KV_RANK, N_HEADS, 2 * LANES)
    wk = wkv[:, :, :MLA_NOPE_DIM].reshape(MLA_KV_RANK, -1)
    wvt = wkv[:, :, MLA_NOPE_DIM:].reshape(MLA_KV_RANK, -1).T
    return w1.astype(BF16), wq2.astype(BF16), wk.astype(BF16), wvt.astype(BF16)


def _mla_proj(x2d, positions, ln, w_in, q_norm, w_qb, kv_norm, w_kvb, tokens_per_batch):
    m, d = x2d.shape
    tm = min(PROJ_ROWS, tokens_per_batch)
    w1, wq2, wk, wvt = _mla_weights(w_in, w_qb, w_kvb)
    half = MLA_ROPE_DIM // 2
    inv_freq = ROPE_BASE ** (-jnp.arange(0, MLA_ROPE_DIM, 2, dtype=F32) / MLA_ROPE_DIM)
    inv4 = jnp.tile(inv_freq, LANES // half).reshape(1, LANES)
    pos = positions.astype(F32).reshape(m, 1)
    return pl.pallas_call(
        _mla_proj_kernel,
        grid=(m // tm,),
        in_specs=[pl.BlockSpec((tm, d), lambda i: (i, 0)),
                  pl.BlockSpec((tm, 1), lambda i: (i, 0)),
                  _resident((1, LANES)), _resident((1, d)), _resident(w1.shape),
                  _resident((1, MLA_Q_RANK)), _resident(wq2.shape),
                  _resident((1, MLA_KV_RANK)), _resident(wk.shape), _resident(wvt.shape)],
        out_specs=[_head_block(tm, 2 * LANES), _head_block(tm, 2 * LANES),
                   pl.BlockSpec((N_HEADS, LANES, tm), lambda i: (0, 0, i)), _head_block(tm, LANES)],
        out_shape=[_head_major(m, 2 * LANES, BF16), _head_major(m, 2 * LANES, BF16),
                   jax.ShapeDtypeStruct((N_HEADS, LANES, m), BF16), _head_major(m, LANES, F32)],
        compiler_params=_params("parallel"),
        name="mla_proj",
    )(x2d, pos, inv4, ln.reshape(1, d), w1, q_norm.reshape(1, -1), wq2, kv_norm.reshape(1, -1), wk, wvt)


def _out_kernel(o_ref, w_ref, x_ref, *rest, final):
    if final:
        g_ref, y_ref = rest
    else:
        (y_ref,) = rest
    o = jnp.concatenate([o_ref[hh] for hh in range(N_HEADS)], axis=1)
    y = x_ref[...] + _dot(o, w_ref[...])
    if final:
        y = _rmsnorm(y, g_ref[...])
    y_ref[...] = y


def _out_proj(o, w_out, x2d, final_g=None):
    m, d = x2d.shape
    tm = min(OUT_ROWS, m)
    final = final_g is not None
    in_specs = [_head_block(tm, LANES), _resident((D_INNER, d)), pl.BlockSpec((tm, d), lambda i: (i, 0))]
    args = [o, w_out, x2d]
    if final:
        in_specs.append(_resident((1, d)))
        args.append(final_g.reshape(1, d))
    return pl.pallas_call(
        functools.partial(_out_kernel, final=final),
        grid=(m // tm,),
        in_specs=in_specs,
        out_specs=pl.BlockSpec((tm, d), lambda i: (i, 0)),
        out_shape=jax.ShapeDtypeStruct((m, d), F32),
        compiler_params=_params("parallel"),
        name="out_proj_final" if final else "out_proj",
    )(*args)


def _stick_kernel(q_ref, k_ref, v_ref, g_ref, suffix_ref, o_ref, acc_ref, c_ref, *, t):
    i = pl.program_id(2)
    n_heads = q_ref.shape[0]
    row = lax.broadcasted_iota(jnp.int32, (t, t), 0)
    col = lax.broadcasted_iota(jnp.int32, (t, t), 1)
    strict = col < row

    def keys(kj):
        return pl.ds(pl.multiple_of(kj * t, t), t)

    def skip_bits(z):
        return jnp.maximum(z, 0.0) + jnp.log2(1.0 + jnp.exp2(-jnp.abs(z)))

    def suffix_sums(u):
        return _dot(jnp.concatenate(_bf16_terms(u, 2), axis=1), suffix_ref[...])

    def row_sum(u):
        return jnp.sum(u, axis=1, keepdims=True)

    prev = jnp.maximum(i - 1, 0)
    heads = range(n_heads)
    z_diag, z_prev, u_diag, u_prev = [], [], [], []
    for hh in heads:
        z_diag.append(_dot_nt(q_ref[hh], k_ref[hh, keys(i), :]))
        z_prev.append(_dot_nt(q_ref[hh], k_ref[hh, keys(prev), :]))
    for hh in heads:
        u_diag.append(jnp.where(strict, skip_bits(z_diag[hh]), 0.0))
        u_prev.append(skip_bits(z_prev[hh]))
    later_diag = [suffix_sums(u_diag[hh]) for hh in heads]
    later_prev = [suffix_sums(u_prev[hh]) for hh in heads]
    state = []
    for hh in heads:
        w = jnp.where(strict, jnp.exp2(z_diag[hh] - later_diag[hh]), 0.0)
        acc = _dot(w.astype(BF16), v_ref[hh, keys(i), :])
        c = -row_sum(u_diag[hh])
        w = jnp.exp2(z_prev[hh] - later_prev[hh] + c)
        vb = v_ref[hh, keys(prev), :]
        acc = acc + _dot(w.astype(BF16), jnp.where(i > 0, vb, jnp.zeros_like(vb)))
        c = c - row_sum(u_prev[hh])
        acc_ref[hh] = acc
        c_ref[hh] = c
        state.append(jnp.max(c))
    c_max = functools.reduce(jnp.maximum, state)

    def more(carry):
        kj, c_max = carry
        return jnp.logical_and(kj >= 0, c_max > -UNDERFLOW_LOG2)

    def body(carry):
        kj, _ = carry
        c_maxes = []
        for hh in heads:
            z = _dot_nt(q_ref[hh], k_ref[hh, keys(kj), :])
            u = skip_bits(z)
            w = jnp.exp2(z - suffix_sums(u) + c_ref[hh])
            acc_ref[hh] += _dot(w.astype(BF16), v_ref[hh, keys(kj), :])
            c = c_ref[hh] - row_sum(u)
            c_ref[hh] = c
            c_maxes.append(jnp.max(c))
        return kj - 1, functools.reduce(jnp.maximum, c_maxes)

    lax.while_loop(more, body, (i - 2, c_max))
    for hh in heads:
        o_ref[hh] = (acc_ref[hh] * g_ref[hh]).astype(BF16)


def _stick_attention(qkv, gate, batch, seq):
    t = min(STICK_BLOCK, seq)
    nq = seq // t
    m = batch * seq
    h = N_HEADS
    hp = STICK_HEADS_PER_STEP
    groups = h // hp
    from_here = jnp.tril(jnp.ones((t, t), BF16))
    suffix_sum = jnp.concatenate([from_here, from_here], axis=0)
    return pl.pallas_call(
        functools.partial(_stick_kernel, t=t),
        grid=(groups, batch, nq),
        in_specs=[
            pl.BlockSpec((hp, t, LANES), lambda hh, b, i: (hh, b * nq + i, 0)),
            pl.BlockSpec((hp, seq, LANES), lambda hh, b, i: (groups + hh, b, 0)),
            pl.BlockSpec((hp, seq, LANES), lambda hh, b, i: (2 * groups + hh, b, 0)),
            pl.BlockSpec((hp, t, LANES), lambda hh, b, i: (hh, b * nq + i, 0)),
            _resident(suffix_sum.shape),
        ],
        out_specs=pl.BlockSpec((hp, t, LANES), lambda hh, b, i: (hh, b * nq + i, 0)),
        out_shape=jax.ShapeDtypeStruct((h, m, LANES), BF16),
        scratch_shapes=[pltpu.VMEM((hp, t, HEAD_DIM), F32), pltpu.VMEM((hp, t, 1), F32)],
        compiler_params=_params("parallel", "parallel", "arbitrary"),
        name="stick_attention",
    )(qkv, qkv, qkv, gate, suffix_sum)


def _softmax_kernel(q_ref, k_ref, vt_ref, g_ref, o_ref, s_ref, p_ref, *, t, chunked):
    n_heads, seq, _ = q_ref.shape
    heads = range(n_heads)
    key = lax.broadcasted_iota(jnp.int32, (KEY_SUB, KEY_SUB), 0)
    qry = lax.broadcasted_iota(jnp.int32, (KEY_SUB, KEY_SUB), 1)
    on_diagonal = (key // CHUNK) <= (qry // CHUNK) if chunked else key <= qry

    subs = t // KEY_SUB

    def key_slice(g):
        if isinstance(g, int):
            return pl.ds(g * KEY_SUB, KEY_SUB)
        return pl.ds(pl.multiple_of(g * KEY_SUB, KEY_SUB), KEY_SUB)

    def softmax_step(hh, slot, diag_chunk, pv_prev, m_run, l_run, alpha_prev, acc):
        if diag_chunk is None:
            m_new = jnp.maximum(m_run, jnp.max(s_ref[hh, slot], axis=0, keepdims=True))
            p = jnp.exp2(s_ref[hh, slot] - m_new)
        else:
            lanes = [slice(c * KEY_SUB, (c + 1) * KEY_SUB) for c in range(subs)]
            pieces = []
            for c in range(subs):
                if c < diag_chunk:
                    pieces.append(jnp.full((KEY_SUB, KEY_SUB), NEG, F32))
                elif c == diag_chunk:
                    pieces.append(jnp.where(on_diagonal, s_ref[hh, slot, :, lanes[c]], NEG))
                else:
                    pieces.append(s_ref[hh, slot, :, lanes[c]])
            s_max = jnp.max(jnp.concatenate(pieces, axis=1), axis=0, keepdims=True)
            m_new = jnp.maximum(m_run, s_max)
            p = jnp.concatenate(
                [jnp.zeros((KEY_SUB, KEY_SUB), F32) if c < diag_chunk
                 else jnp.exp2(pieces[c] - m_new[:, lanes[c]]) for c in range(subs)], axis=1)
        alpha = jnp.exp2(m_run - m_new)
        p_ref[hh, slot] = p.astype(BF16)
        l_run = alpha * l_run + jnp.sum(p, axis=0, keepdims=True)
        if pv_prev is not None:
            acc = alpha_prev * acc + pv_prev
        return m_new, l_run, alpha, acc

    n_tiles = seq // t

    def scores(hh, qi, g, slot):
        rows = slice(qi * t, (qi + 1) * t)
        s_ref[hh, slot] = _dot_nt(k_ref[hh, key_slice(g), :], q_ref[hh, rows, :])

    def weighted_values(hh, g, slot):
        return _dot(vt_ref[hh, :, key_slice(g)], p_ref[hh, slot])

    for hh in heads:
        scores(hh, 0, subs - 1, 0)

    for qi in range(n_tiles):
        rows = slice(qi * t, (qi + 1) * t)
        top = (qi + 1) * subs - 1

        def tile(first, states, diagonal, last):
            for j in range(subs):
                g, slot = first - j, j % 2
                diag_chunk = subs - 1 - j if diagonal else None
                for hh in heads:
                    if not (last and j == subs - 1):
                        scores(hh, qi, g - 1, 1 - slot)
                    elif qi + 1 < n_tiles:
                        scores(hh, qi + 1, top + subs, 1 - slot)
                if diagonal and j == 0:
                    pv_prev = [None] * n_heads
                else:
                    pv_prev = [weighted_values(hh, g + 1, 1 - slot) for hh in heads]
                states = tuple(softmax_step(hh, slot, diag_chunk, pv_prev[hh], *states[hh]) for hh in heads)
            return states

        init = (jnp.full((1, t), NEG, F32), jnp.zeros((1, t), F32), jnp.ones((1, t), F32),
                jnp.zeros((HEAD_DIM, t), F32))
        states = tile(top, (init,) * n_heads, True, qi == 0)
        if qi > 1:
            states = lax.fori_loop(
                0, qi - 1, lambda n, st: tile(top - subs * (n + 1), st, False, False), states)
        if qi > 0:
            states = tile(subs - 1, states, False, True)
        for hh in heads:
            _, l_run, alpha, acc = states[hh]
            acc = alpha * acc + weighted_values(hh, 0, 1)
            o_ref[hh, rows, :] = ((acc * (1.0 / l_run)).T * g_ref[hh, rows, :]).astype(BF16)


def _softmax_attention(qcat, kcat, vt, gate, batch, seq, chunked):
    t = min(SOFTMAX_TILE, seq)
    m = batch * seq
    h = N_HEADS
    width = qcat.shape[-1]
    hp = HEADS_PER_STEP
    tokens = lambda w: pl.BlockSpec((hp, seq, w), lambda hh, b: (hh, b, 0))
    return pl.pallas_call(
        functools.partial(_softmax_kernel, t=t, chunked=chunked),
        grid=(h // hp, batch),
        in_specs=[tokens(width), tokens(width),
                  pl.BlockSpec((hp, LANES, seq), lambda hh, b: (hh, 0, b)), tokens(LANES)],
        out_specs=tokens(LANES),
        scratch_shapes=[pltpu.VMEM((hp, 2, min(KEY_SUB, t), t), F32),
                        pltpu.VMEM((hp, 2, min(KEY_SUB, t), t), BF16)],
        out_shape=jax.ShapeDtypeStruct((h, m, LANES), BF16),
        compiler_params=_params("parallel", "parallel"),
        name="mla_attention" if chunked else "forget_attention",
    )(qcat, kcat, vt, gate)


def kernel(x, positions, ln0, w_in0, w_out0, ln1, w_in1, q_norm1, w_qb1, kv_norm1, w_kvb1, w_out1,
           ln2, w_in2, b_f2, w_out2, ln3, w_in3, w_out3, final_norm):
    batch, seq, d = x.shape
    x2d = x.reshape(batch * seq, d)

    qkv, gate = _norm_proj(x2d, ln0, w_in0.astype(BF16), seq)
    o = _stick_attention(qkv, gate, batch, seq)
    x2d = _out_proj(o, w_out0.astype(BF16), x2d)

    qcat, kcat, vt, gate = _mla_proj(x2d, positions, ln1, w_in1, q_norm1, w_qb1, kv_norm1, w_kvb1, seq)
    o = _softmax_attention(qcat, kcat, vt, gate, batch, seq, chunked=True)
    x2d = _out_proj(o, w_out1.astype(BF16), x2d)

    qcat, kcat, vt, gate = _forget_proj(x2d, ln2, w_in2, b_f2, seq)
    o = _softmax_attention(qcat, kcat, vt, gate, batch, seq, chunked=False)
    x2d = _out_proj(o, w_out2.astype(BF16), x2d)

    qkv, gate = _norm_proj(x2d, ln3, w_in3.astype(BF16), seq)
    o = _stick_attention(qkv, gate, batch, seq)
    x2d = _out_proj(o, w_out3.astype(BF16), x2d, final_g=final_norm)
    return x2d.reshape(batch, seq, d)
```

```python
import functools

import jax
import jax.numpy as jnp
from jax import lax
from jax.experimental import pallas as pl
from jax.experimental.pallas import tpu as pltpu

N_HEADS = 16
HEAD_DIM = 128
D_INNER = N_HEADS * HEAD_DIM
CHUNK = 64
MLA_Q_RANK = 256
MLA_KV_RANK = 128
MLA_NOPE_DIM = 128
MLA_ROPE_DIM = 64
ROPE_BASE = 10000.0
EPS = 1e-6
NEG = -1e30
LOG2E = 1.4426950408889634
UNDERFLOW_LOG2 = 150.0

PROJ_ROWS = 256
OUT_ROWS = 512
STICK_BLOCK = 256
STICK_HEADS_PER_STEP = 8
SOFTMAX_TILE = 1024
KEY_SUB = 512
HEADS_PER_STEP = 2
LANES = 128
VMEM_LIMIT_BYTES = 56 * 1024 * 1024

F32 = jnp.float32
BF16 = jnp.bfloat16


def _params(*semantics):
    return pltpu.CompilerParams(dimension_semantics=semantics,
                                vmem_limit_bytes=VMEM_LIMIT_BYTES)


def _resident(shape):
    zeros = (0,) * len(shape)
    return pl.BlockSpec(shape, lambda *_: zeros, pipeline_mode=pl.Buffered(1))


def _rmsnorm(x, g):
    return x * lax.rsqrt(jnp.mean(x * x, axis=-1, keepdims=True) + EPS) * g


def _dot(a, b):
    return jnp.dot(a, b, preferred_element_type=F32)


def _dot_nt(a, b):
    return lax.dot_general(a, b, (((1,), (1,)), ((), ())), preferred_element_type=F32)


def _bf16_terms(a, terms):
    out = []
    for t in range(terms):
        part = a.astype(BF16)
        out.append(part)
        if t + 1 < terms:
            a = a - part.astype(F32)
    return out


def _silu(g):
    return g * (1.0 / (1.0 + jnp.exp(-g)))


def _head_major(m, width, dtype):
    return jax.ShapeDtypeStruct((N_HEADS, m, width), dtype)


def _head_block(tm, width):
    return pl.BlockSpec((N_HEADS, tm, width), lambda i: (0, i, 0))


def _proj_kernel(x_ref, g_ref, w_ref, qkv_ref, gate_ref):
    h = _rmsnorm(x_ref[...], g_ref[...]).astype(BF16)
    n_bf16 = 3 * N_HEADS
    step = 8
    for c0 in range(0, 4 * N_HEADS, step):
        acc = _dot(h, w_ref[:, c0 * LANES:(c0 + step) * LANES])
        for c in range(step):
            piece = acc[:, c * LANES:(c + 1) * LANES]
            if c0 + c < N_HEADS:
                qkv_ref[c0 + c] = (piece * (HEAD_DIM ** -0.5 * LOG2E)).astype(BF16)
            elif c0 + c < n_bf16:
                qkv_ref[c0 + c] = piece.astype(BF16)
            else:
                gate_ref[c0 + c - n_bf16] = _silu(piece)


def _norm_proj(x2d, g, w, tokens_per_batch):
    m, d = x2d.shape
    tm = min(PROJ_ROWS, tokens_per_batch)
    n_bf16 = 3 * N_HEADS
    return pl.pallas_call(
        _proj_kernel,
        grid=(m // tm,),
        in_specs=[pl.BlockSpec((tm, d), lambda i: (i, 0)), _resident((1, d)), _resident(w.shape)],
        out_specs=[pl.BlockSpec((n_bf16, tm, LANES), lambda i: (0, i, 0)), _head_block(tm, LANES)],
        out_shape=[jax.ShapeDtypeStruct((n_bf16, m, LANES), BF16), _head_major(m, LANES, F32)],
        compiler_params=_params("parallel"),
        name="norm_proj",
    )(x2d, g.reshape(1, d), w)


def _forget_proj_kernel(x_ref, g_ref, w_ref, wvt_ref, wf_ref, bf_ref,
                        qcat_ref, kcat_ref, vt_ref, gate_ref, carry_ref, *, per_batch):
    i = pl.program_id(0)
    tm = x_ref.shape[0]
    h = _rmsnorm(x_ref[...], g_ref[...]).astype(BF16)

    @pl.when(i % per_batch == 0)
    def _():
        carry_ref[...] = jnp.zeros_like(carry_ref)

    f = _dot(h, wf_ref[...]) + bf_ref[...]
    log_f = (jnp.minimum(f, 0.0) - jnp.log1p(jnp.exp(-jnp.abs(f)))) * LOG2E
    row = lax.broadcasted_iota(jnp.int32, (tm, tm), 0)
    col = lax.broadcasted_iota(jnp.int32, (tm, tm), 1)
    up_to = (col <= row).astype(BF16)
    cf = carry_ref[0:1, :]
    for part in _bf16_terms(log_f, 3):
        cf = cf + _dot(up_to, part)
    carry_ref[...] = jnp.broadcast_to(cf[tm - 1:tm, :], carry_ref.shape)

    hi, mid, lo = (part.astype(F32) for part in _bf16_terms(cf, 3))
    lane = lax.broadcasted_iota(jnp.int32, (1, LANES), 1)
    for hh in range(N_HEADS):
        terms = jnp.where(lane % 3 == 0, hi[:, hh:hh + 1],
                          jnp.where(lane % 3 == 1, mid[:, hh:hh + 1], lo[:, hh:hh + 1]))
        q_ext = jnp.where(lane < 3, terms, jnp.where(lane < 6, 1.0, 0.0))
        k_ext = jnp.where(lane < 3, 1.0, jnp.where(lane < 6, -terms, 0.0))
        qcat_ref[hh, :, LANES:] = q_ext.astype(BF16)
        kcat_ref[hh, :, LANES:] = k_ext.astype(BF16)

    step = 8
    for c0 in range(0, 3 * N_HEADS, step):
        acc = _dot(h, w_ref[:, c0 * LANES:(c0 + step) * LANES])
        for c in range(step):
            piece = acc[:, c * LANES:(c + 1) * LANES]
            g_idx = c0 + c
            if g_idx < N_HEADS:
                qcat_ref[g_idx, :, :LANES] = (piece * (HEAD_DIM ** -0.5 * LOG2E)).astype(BF16)
            elif g_idx < 2 * N_HEADS:
                kcat_ref[g_idx - N_HEADS, :, :LANES] = piece.astype(BF16)
            else:
                gate_ref[g_idx - 2 * N_HEADS] = _silu(piece)
    for c0 in range(0, N_HEADS, step):
        vt = _dot_nt(wvt_ref[c0 * LANES:(c0 + step) * LANES, :], h)
        for c in range(step):
            vt_ref[c0 + c] = vt[c * LANES:(c + 1) * LANES, :].astype(BF16)


def _forget_proj(x2d, g, w_in, b_f, tokens_per_batch):
    m, d = x2d.shape
    tm = min(PROJ_ROWS, tokens_per_batch)
    di = D_INNER
    w = jnp.concatenate([w_in[:, :2 * di], w_in[:, 3 * di:4 * di]], axis=1).astype(BF16)
    wvt = w_in[:, 2 * di:3 * di].T.astype(BF16)
    wf = jnp.pad(w_in[:, 4 * di:], ((0, 0), (0, LANES - N_HEADS))).astype(BF16)
    bf = jnp.pad(b_f, (0, LANES - N_HEADS)).reshape(1, LANES)
    return pl.pallas_call(
        functools.partial(_forget_proj_kernel, per_batch=tokens_per_batch // tm),
        grid=(m // tm,),
        in_specs=[pl.BlockSpec((tm, d), lambda i: (i, 0)), _resident((1, d)), _resident(w.shape),
                  _resident(wvt.shape), _resident(wf.shape), _resident(bf.shape)],
        out_specs=[_head_block(tm, 2 * LANES), _head_block(tm, 2 * LANES),
                   pl.BlockSpec((N_HEADS, LANES, tm), lambda i: (0, 0, i)), _head_block(tm, LANES)],
        out_shape=[_head_major(m, 2 * LANES, BF16), _head_major(m, 2 * LANES, BF16),
                   jax.ShapeDtypeStruct((N_HEADS, LANES, m), BF16), _head_major(m, LANES, F32)],
        scratch_shapes=[pltpu.VMEM((8, LANES), F32)],
        compiler_params=_params("arbitrary"),
        name="forget_proj",
    )(x2d, g.reshape(1, d), w, wvt, wf, bf)


def _mla_proj_kernel(x_ref, pos_ref, inv_ref, g_ref, w1_ref, qn_ref, wq_ref, kvn_ref, wk_ref, wvt_ref,
                     qcat_ref, kcat_ref, vt_ref, gate_ref):
    h = _rmsnorm(x_ref[...], g_ref[...]).astype(BF16)
    lat = _dot(h, w1_ref[:, :5 * LANES])
    q_lat = lat[:, :MLA_Q_RANK]
    kv_lat = lat[:, MLA_Q_RANK:MLA_Q_RANK + MLA_KV_RANK]
    k_a = lat[:, 3 * LANES:4 * LANES]
    k_b = lat[:, 4 * LANES:5 * LANES]

    lane = lax.broadcasted_iota(jnp.int32, (1, LANES), 1)
    half = MLA_ROPE_DIM // 2
    ang = pos_ref[...] * inv_ref[...]
    cos = jnp.cos(ang)
    sin = jnp.where((lane // half) % 2 == 0, -1.0, 1.0) * jnp.sin(ang)
    k_rope = (k_a * cos + k_b * sin).astype(BF16)

    kvn = _rmsnorm(kv_lat, kvn_ref[...]).astype(BF16)
    step = 8
    for c0 in range(0, N_HEADS, step):
        k_nope = _dot(kvn, wk_ref[:, c0 * LANES:(c0 + step) * LANES])
        vt = _dot_nt(wvt_ref[c0 * LANES:(c0 + step) * LANES, :], kvn)
        for c in range(step):
            kcat_ref[c0 + c] = jnp.concatenate(
                [k_nope[:, c * LANES:(c + 1) * LANES].astype(BF16), k_rope], axis=1)
            vt_ref[c0 + c] = vt[c * LANES:(c + 1) * LANES, :].astype(BF16)

    qn = _rmsnorm(q_lat, qn_ref[...]).astype(BF16)
    n_pairs = N_HEADS // 2
    rope_a = _dot(qn, wq_ref[:, N_HEADS * LANES:(N_HEADS + n_pairs) * LANES])
    rope_b = _dot(qn, wq_ref[:, (N_HEADS + n_pairs) * LANES:])
    first_head = lane < MLA_ROPE_DIM
    q_scale = (MLA_NOPE_DIM + MLA_ROPE_DIM) ** -0.5 * LOG2E
    for c0 in range(0, N_HEADS, step):
        nope = _dot(qn, wq_ref[:, c0 * LANES:(c0 + step) * LANES])
        for c in range(step):
            hh = c0 + c
            pair = hh // 2
            sl = slice(pair * LANES, (pair + 1) * LANES)
            q_rope = rope_a[:, sl] * cos + rope_b[:, sl] * sin
            mine = first_head if hh % 2 == 0 else jnp.logical_not(first_head)
            q_rope = jnp.where(mine, q_rope * q_scale, 0.0)
            q_nope = nope[:, c * LANES:(c + 1) * LANES] * q_scale
            qcat_ref[hh] = jnp.concatenate([q_nope.astype(BF16), q_rope.astype(BF16)], axis=1)

    for c0 in range(0, N_HEADS, step):
        gt = _dot(h, w1_ref[:, (5 + c0) * LANES:(5 + c0 + step) * LANES])
        for c in range(step):
            gate_ref[c0 + c] = _silu(gt[:, c * LANES:(c + 1) * LANES])


def _mla_weights(w_in, w_qb, w_kvb):
    i1 = MLA_Q_RANK
    i2 = i1 + MLA_KV_RANK
    i3 = i2 + MLA_ROPE_DIM
    half = MLA_ROPE_DIM // 2
    wk1, wk2 = w_in[:, i2:i2 + half], w_in[:, i2 + half:i3]
    w1 = jnp.concatenate([w_in[:, :i2], wk1, wk2, wk1, wk2, wk2, wk1, wk2, wk1, w_in[:, i3:]], axis=1)
    wq = w_qb.reshape(MLA_Q_RANK, N_HEADS, MLA_NOPE_DIM + MLA_ROPE_DIM)
    nope = wq[:, :, :MLA_NOPE_DIM].reshape(MLA_Q_RANK, -1)
    r1 = wq[:, :, MLA_NOPE_DIM:MLA_NOPE_DIM + half]
    r2 = wq[:, :, MLA_NOPE_DIM + half:]
    rope_a = jnp.concatenate([r1, r2], axis=2).reshape(MLA_Q_RANK, -1)
    rope_b = jnp.concatenate([r2, r1], axis=2).reshape(MLA_Q_RANK, -1)
    wq2 = jnp.concatenate([nope, rope_a, rope_b], axis=1)
    wkv = w_kvb.reshape(MLA_KV_RANK, N_HEADS, 2 * LANES)
    wk = wkv[:, :, :MLA_NOPE_DIM].reshape(MLA_KV_RANK, -1)
    wvt = wkv[:, :, MLA_NOPE_DIM:].reshape(MLA_KV_RANK, -1).T
    return w1.astype(BF16), wq2.astype(BF16), wk.astype(BF16), wvt.astype(BF16)


def _mla_proj(x2d, positions, ln, w_in, q_norm, w_qb, kv_norm, w_kvb, tokens_per_batch):
    m, d = x2d.shape
    tm = min(PROJ_ROWS, tokens_per_batch)
    w1, wq2, wk, wvt = _mla_weights(w_in, w_qb, w_kvb)
    half = MLA_ROPE_DIM // 2
    inv_freq = ROPE_BASE ** (-jnp.arange(0, MLA_ROPE_DIM, 2, dtype=F32) / MLA_ROPE_DIM)
    inv4 = jnp.tile(inv_freq, LANES // half).reshape(1, LANES)
    pos = positions.astype(F32).reshape(m, 1)
    return pl.pallas_call(
        _mla_proj_kernel,
        grid=(m // tm,),
        in_specs=[pl.BlockSpec((tm, d), lambda i: (i, 0)),
                  pl.BlockSpec((tm, 1), lambda i: (i, 0)),
                  _resident((1, LANES)), _resident((1, d)), _resident(w1.shape),
                  _resident((1, MLA_Q_RANK)), _resident(wq2.shape),
                  _resident((1, MLA_KV_RANK)), _resident(wk.shape), _resident(wvt.shape)],
        out_specs=[_head_block(tm, 2 * LANES), _head_block(tm, 2 * LANES),
                   pl.BlockSpec((N_HEADS, LANES, tm), lambda i: (0, 0, i)), _head_block(tm, LANES)],
        out_shape=[_head_major(m, 2 * LANES, BF16), _head_major(m, 2 * LANES, BF16),
                   jax.ShapeDtypeStruct((N_HEADS, LANES, m), BF16), _head_major(m, LANES, F32)],
        compiler_params=_params("parallel"),
        name="mla_proj",
    )(x2d, pos, inv4, ln.reshape(1, d), w1, q_norm.reshape(1, -1), wq2, kv_norm.reshape(1, -1), wk, wvt)


def _out_kernel(o_ref, w_ref, x_ref, *rest, final):
    if final:
        g_ref, y_ref = rest
    else:
        (y_ref,) = rest
    o = jnp.concatenate([o_ref[hh] for hh in range(N_HEADS)], axis=1)
    y = x_ref[...] + _dot(o, w_ref[...])
    if final:
        y = _rmsnorm(y, g_ref[...])
    y_ref[...] = y


def _out_proj(o, w_out, x2d, final_g=None):
    m, d = x2d.shape
    tm = min(OUT_ROWS, m)
    final = final_g is not None
    in_specs = [_head_block(tm, LANES), _resident((D_INNER, d)), pl.BlockSpec((tm, d), lambda i: (i, 0))]
    args = [o, w_out, x2d]
    if final:
        in_specs.append(_resident((1, d)))
        args.append(final_g.reshape(1, d))
    return pl.pallas_call(
        functools.partial(_out_kernel, final=final),
        grid=(m // tm,),
        in_specs=in_specs,
        out_specs=pl.BlockSpec((tm, d), lambda i: (i, 0)),
        out_shape=jax.ShapeDtypeStruct((m, d), F32),
        compiler_params=_params("parallel"),
        name="out_proj_final" if final else "out_proj",
    )(*args)


def _stick_kernel(q_ref, k_ref, v_ref, g_ref, suffix_ref, o_ref, acc_ref, c_ref, *, t):
    i = pl.program_id(2)
    n_heads = q_ref.shape[0]
    row = lax.broadcasted_iota(jnp.int32, (t, t), 0)
    col = lax.broadcasted_iota(jnp.int32, (t, t), 1)
    strict = col < row

    def keys(kj):
        return pl.ds(pl.multiple_of(kj * t, t), t)

    def skip_bits(z):
        return jnp.maximum(z, 0.0) + jnp.log2(1.0 + jnp.exp2(-jnp.abs(z)))

    def suffix_sums(u):
        return _dot(jnp.concatenate(_bf16_terms(u, 2), axis=1), suffix_ref[...])

    def row_sum(u):
        return jnp.sum(u, axis=1, keepdims=True)

    prev = jnp.maximum(i - 1, 0)
    heads = range(n_heads)
    top = (3 * t) // 4
    z_diag, z_prev, u_diag, u_prev = [], [], [], []
    for hh in heads:
        z_diag.append(_dot_nt(q_ref[hh], k_ref[hh, keys(i), :]))
        z_prev.append(_dot_nt(q_ref[hh, :top, :], k_ref[hh, keys(prev), :]))
    for hh in heads:
        u_diag.append(jnp.where(strict, skip_bits(z_diag[hh]), 0.0))
        u_prev.append(skip_bits(z_prev[hh]))
    later_diag = [suffix_sums(u_diag[hh]) for hh in heads]
    later_prev = [suffix_sums(u_prev[hh]) for hh in heads]
    top_max, rest_max = [], []
    for hh in heads:
        w = jnp.where(strict, jnp.exp2(z_diag[hh] - later_diag[hh]), 0.0)
        acc = _dot(w.astype(BF16), v_ref[hh, keys(i), :])
        c = -row_sum(u_diag[hh])
        w = jnp.exp2(z_prev[hh] - later_prev[hh] + c[:top])
        vb = v_ref[hh, keys(prev), :]
        c_top = c[:top] - row_sum(u_prev[hh])
        acc_ref[hh, :top] = acc[:top] + _dot(w.astype(BF16), jnp.where(i > 0, vb, jnp.zeros_like(vb)))
        acc_ref[hh, top:] = acc[top:]
        c_ref[hh, :top] = c_top
        c_ref[hh, top:] = c[top:]
        top_max.append(jnp.max(c_top))
        rest_max.append(jnp.max(c[top:]))
    rest_max = functools.reduce(jnp.maximum, rest_max)
    c_max = jnp.maximum(functools.reduce(jnp.maximum, top_max), rest_max)
    start = jnp.where(rest_max > -UNDERFLOW_LOG2, i - 1, i - 2)
    below_top = lax.broadcasted_iota(jnp.int32, (t, 1), 0) >= top

    def more(carry):
        kj, c_max = carry
        return jnp.logical_and(kj >= 0, c_max > -UNDERFLOW_LOG2)

    def body(carry):
        kj, _ = carry
        todo = jnp.logical_or(kj != i - 1, below_top)
        c_maxes = []
        for hh in heads:
            z = _dot_nt(q_ref[hh], k_ref[hh, keys(kj), :])
            u = jnp.where(todo, skip_bits(z), 0.0)
            w = jnp.where(todo, jnp.exp2(z - suffix_sums(u) + c_ref[hh]), 0.0)
            acc_ref[hh] += _dot(w.astype(BF16), v_ref[hh, keys(kj), :])
            c = c_ref[hh] - row_sum(u)
            c_ref[hh] = c
            c_maxes.append(jnp.max(c))
        return kj - 1, functools.reduce(jnp.maximum, c_maxes)

    lax.while_loop(more, body, (start, c_max))
    for hh in heads:
        o_ref[hh] = (acc_ref[hh] * g_ref[hh]).astype(BF16)


def _stick_attention(qkv, gate, batch, seq):
    t = min(STICK_BLOCK, seq)
    nq = seq // t
    m = batch * seq
    h = N_HEADS
    hp = STICK_HEADS_PER_STEP
    groups = h // hp
    from_here = jnp.tril(jnp.ones((t, t), BF16))
    suffix_sum = jnp.concatenate([from_here, from_here], axis=0)
    return pl.pallas_call(
        functools.partial(_stick_kernel, t=t),
        grid=(groups, batch, nq),
        in_specs=[
            pl.BlockSpec((hp, t, LANES), lambda hh, b, i: (hh, b * nq + i, 0)),
            pl.BlockSpec((hp, seq, LANES), lambda hh, b, i: (groups + hh, b, 0)),
            pl.BlockSpec((hp, seq, LANES), lambda hh, b, i: (2 * groups + hh, b, 0)),
            pl.BlockSpec((hp, t, LANES), lambda hh, b, i: (hh, b * nq + i, 0)),
            _resident(suffix_sum.shape),
        ],
        out_specs=pl.BlockSpec((hp, t, LANES), lambda hh, b, i: (hh, b * nq + i, 0)),
        out_shape=jax.ShapeDtypeStruct((h, m, LANES), BF16),
        scratch_shapes=[pltpu.VMEM((hp, t, HEAD_DIM), F32), pltpu.VMEM((hp, t, 1), F32)],
        compiler_params=_params("parallel", "parallel", "arbitrary"),
        name="stick_attention",
    )(qkv, qkv, qkv, gate, suffix_sum)


def _softmax_kernel(q_ref, k_ref, vt_ref, g_ref, o_ref, s_ref, p_ref, *, t, chunked):
    n_heads, seq, _ = q_ref.shape
    heads = range(n_heads)
    key = lax.broadcasted_iota(jnp.int32, (KEY_SUB, KEY_SUB), 0)
    qry = lax.broadcasted_iota(jnp.int32, (KEY_SUB, KEY_SUB), 1)
    on_diagonal = (key // CHUNK) <= (qry // CHUNK) if chunked else key <= qry

    subs = t // KEY_SUB

    def key_slice(g):
        if isinstance(g, int):
            return pl.ds(g * KEY_SUB, KEY_SUB)
        return pl.ds(pl.multiple_of(g * KEY_SUB, KEY_SUB), KEY_SUB)

    def softmax_step(hh, slot, diag_chunk, pv_prev, m_run, l_run, alpha_prev, acc):
        if diag_chunk is None:
            m_new = jnp.maximum(m_run, jnp.max(s_ref[hh, slot], axis=0, keepdims=True))
            p = jnp.exp2(s_ref[hh, slot] - m_new)
        else:
            lanes = [slice(c * KEY_SUB, (c + 1) * KEY_SUB) for c in range(subs)]
            pieces = []
            for c in range(subs):
                if c < diag_chunk:
                    pieces.append(jnp.full((KEY_SUB, KEY_SUB), NEG, F32))
                elif c == diag_chunk:
                    pieces.append(jnp.where(on_diagonal, s_ref[hh, slot, :, lanes[c]], NEG))
                else:
                    pieces.append(s_ref[hh, slot, :, lanes[c]])
            s_max = jnp.max(jnp.concatenate(pieces, axis=1), axis=0, keepdims=True)
            m_new = jnp.maximum(m_run, s_max)
            p = jnp.concatenate(
                [jnp.zeros((KEY_SUB, KEY_SUB), F32) if c < diag_chunk
                 else jnp.exp2(pieces[c] - m_new[:, lanes[c]]) for c in range(subs)], axis=1)
        alpha = jnp.exp2(m_run - m_new)
        p_ref[hh, slot] = p.astype(BF16)
        l_run = alpha * l_run + jnp.sum(p, axis=0, keepdims=True)
        if pv_prev is not None:
            acc = alpha_prev * acc + pv_prev
        return m_new, l_run, alpha, acc

    n_tiles = seq // t

    def scores(hh, qi, g, slot):
        rows = slice(qi * t, (qi + 1) * t)
        s_ref[hh, slot] = _dot_nt(k_ref[hh, key_slice(g), :], q_ref[hh, rows, :])

    def weighted_values(hh, g, slot):
        return _dot(vt_ref[hh, :, key_slice(g)], p_ref[hh, slot])

    for hh in heads:
        scores(hh, 0, subs - 1, 0)

    for qi in range(n_tiles):
        rows = slice(qi * t, (qi + 1) * t)
        top = (qi + 1) * subs - 1

        def tile(first, states, diagonal, last):
            for j in range(subs):
                g, slot = first - j, j % 2
                diag_chunk = subs - 1 - j if diagonal else None
                for hh in heads:
                    if not (last and j == subs - 1):
                        scores(hh, qi, g - 1, 1 - slot)
                    elif qi + 1 < n_tiles:
                        scores(hh, qi + 1, top + subs, 1 - slot)
                if diagonal and j == 0:
                    pv_prev = [None] * n_heads
                else:
                    pv_prev = [weighted_values(hh, g + 1, 1 - slot) for hh in heads]
                states = tuple(softmax_step(hh, slot, diag_chunk, pv_prev[hh], *states[hh]) for hh in heads)
            return states

        init = (jnp.full((1, t), NEG, F32), jnp.zeros((1, t), F32), jnp.ones((1, t), F32),
                jnp.zeros((HEAD_DIM, t), F32))
        states = tile(top, (init,) * n_heads, True, qi == 0)
        if qi > 1:
            states = lax.fori_loop(
                0, qi - 1, lambda n, st: tile(top - subs * (n + 1), st, False, False), states)
        if qi > 0:
            states = tile(subs - 1, states, False, True)
        for hh in heads:
            _, l_run, alpha, acc = states[hh]
            acc = alpha * acc + weighted_values(hh, 0, 1)
            o_ref[hh, rows, :] = ((acc * (1.0 / l_run)).T * g_ref[hh, rows, :]).astype(BF16)


def _softmax_attention(qcat, kcat, vt, gate, batch, seq, chunked):
    t = min(SOFTMAX_TILE, seq)
    m = batch * seq
    h = N_HEADS
    width = qcat.shape[-1]
    hp = HEADS_PER_STEP
    tokens = lambda w: pl.BlockSpec((hp, seq, w), lambda hh, b: (hh, b, 0))
    return pl.pallas_call(
        functools.partial(_softmax_kernel, t=t, chunked=chunked),
        grid=(h // hp, batch),
        in_specs=[tokens(width), tokens(width),
                  pl.BlockSpec((hp, LANES, seq), lambda hh, b: (hh, 0, b)), tokens(LANES)],
        out_specs=tokens(LANES),
        scratch_shapes=[pltpu.VMEM((hp, 2, min(KEY_SUB, t), t), F32),
                        pltpu.VMEM((hp, 2, min(KEY_SUB, t), t), BF16)],
        out_shape=jax.ShapeDtypeStruct((h, m, LANES), BF16),
        compiler_params=_params("parallel", "parallel"),
        name="mla_attention" if chunked else "forget_attention",
    )(qcat, kcat, vt, gate)


def kernel(x, positions, ln0, w_in0, w_out0, ln1, w_in1, q_norm1, w_qb1, kv_norm1, w_kvb1, w_out1,
           ln2, w_in2, b_f2, w_out2, ln3, w_in3, w_out3, final_norm):
    batch, seq, d = x.shape
    x2d = x.reshape(batch * seq, d)

    qkv, gate = _norm_proj(x2d, ln0, w_in0.astype(BF16), seq)
    o = _stick_attention(qkv, gate, batch, seq)
    x2d = _out_proj(o, w_out0.astype(BF16), x2d)

    qcat, kcat, vt, gate = _mla_proj(x2d, positions, ln1, w_in1, q_norm1, w_qb1, kv_norm1, w_kvb1, seq)
    o = _softmax_attention(qcat, kcat, vt, gate, batch, seq, chunked=True)
    x2d = _out_proj(o, w_out1.astype(BF16), x2d)

    qcat, kcat, vt, gate = _forget_proj(x2d, ln2, w_in2, b_f2, seq)
    o = _softmax_attention(qcat, kcat, vt, gate, batch, seq, chunked=False)
    x2d = _out_proj(o, w_out2.astype(BF16), x2d)

    qkv, gate = _norm_proj(x2d, ln3, w_in3.astype(BF16), seq)
    o = _stick_attention(qkv, gate, batch, seq)
    x2d = _out_proj(o, w_out3.astype(BF16), x2d, final_g=final_norm)
    return x2d.reshape(batch, seq, d)
```

```python
import functools

import jax
import jax.numpy as jnp
from jax import lax
from jax.experimental import pallas as pl
from jax.experimental.pallas import tpu as pltpu

N_HEADS = 16
HEAD_DIM = 128
D_INNER = N_HEADS * HEAD_DIM
CHUNK = 64
MLA_Q_RANK = 256
MLA_KV_RANK = 128
MLA_NOPE_DIM = 128
MLA_ROPE_DIM = 64
ROPE_BASE = 10000.0
EPS = 1e-6
NEG = -1e30
LOG2E = 1.4426950408889634
UNDERFLOW_LOG2 = 150.0

PROJ_ROWS = 256
OUT_ROWS = 512
STICK_BLOCK = 256
STICK_HEADS_PER_STEP = 8
SOFTMAX_TILE = 1024
KEY_SUB = 512
HEADS_PER_STEP = 2
LANES = 128
VMEM_LIMIT_BYTES = 56 * 1024 * 1024

F32 = jnp.float32
BF16 = jnp.bfloat16


def _params(*semantics):
    return pltpu.CompilerParams(dimension_semantics=semantics,
                                vmem_limit_bytes=VMEM_LIMIT_BYTES)


def _resident(shape):
    zeros = (0,) * len(shape)
    return pl.BlockSpec(shape, lambda *_: zeros, pipeline_mode=pl.Buffered(1))


def _rmsnorm(x, g):
    return x * lax.rsqrt(jnp.mean(x * x, axis=-1, keepdims=True) + EPS) * g


def _dot(a, b):
    return jnp.dot(a, b, preferred_element_type=F32)


def _dot_nt(a, b):
    return lax.dot_general(a, b, (((1,), (1,)), ((), ())), preferred_element_type=F32)


def _bf16_terms(a, terms):
    out = []
    for t in range(terms):
        part = a.astype(BF16)
        out.append(part)
        if t + 1 < terms:
            a = a - part.astype(F32)
    return out


def _silu(g):
    return g * (1.0 / (1.0 + jnp.exp(-g)))


def _head_major(m, width, dtype):
    return jax.ShapeDtypeStruct((N_HEADS, m, width), dtype)


def _head_block(tm, width):
    return pl.BlockSpec((N_HEADS, tm, width), lambda i: (0, i, 0))


def _proj_kernel(x_ref, g_ref, w_ref, qkv_ref, gate_ref):
    h = _rmsnorm(x_ref[...], g_ref[...]).astype(BF16)
    n_bf16 = 3 * N_HEADS
    step = 8
    for c0 in range(0, 4 * N_HEADS, step):
        acc = _dot(h, w_ref[:, c0 * LANES:(c0 + step) * LANES])
        for c in range(step):
            piece = acc[:, c * LANES:(c + 1) * LANES]
            if c0 + c < N_HEADS:
                qkv_ref[c0 + c] = (piece * (HEAD_DIM ** -0.5 * LOG2E)).astype(BF16)
            elif c0 + c < n_bf16:
                qkv_ref[c0 + c] = piece.astype(BF16)
            else:
                gate_ref[c0 + c - n_bf16] = _silu(piece)


def _norm_proj(x2d, g, w, tokens_per_batch):
    m, d = x2d.shape
    tm = min(PROJ_ROWS, tokens_per_batch)
    n_bf16 = 3 * N_HEADS
    return pl.pallas_call(
        _proj_kernel,
        grid=(m // tm,),
        in_specs=[pl.BlockSpec((tm, d), lambda i: (i, 0)), _resident((1, d)), _resident(w.shape)],
        out_specs=[pl.BlockSpec((n_bf16, tm, LANES), lambda i: (0, i, 0)), _head_block(tm, LANES)],
        out_shape=[jax.ShapeDtypeStruct((n_bf16, m, LANES), BF16), _head_major(m, LANES, F32)],
        compiler_params=_params("parallel"),
        name="norm_proj",
    )(x2d, g.reshape(1, d), w)


def _forget_proj_kernel(x_ref, g_ref, w_ref, wvt_ref, wf_ref, bf_ref,
                        qcat_ref, kcat_ref, vt_ref, gate_ref, carry_ref, *, per_batch):
    i = pl.program_id(0)
    tm = x_ref.shape[0]
    h = _rmsnorm(x_ref[...], g_ref[...]).astype(BF16)

    @pl.when(i % per_batch == 0)
    def _():
        carry_ref[...] = jnp.zeros_like(carry_ref)

    f = _dot(h, wf_ref[...]) + bf_ref[...]
    log_f = (jnp.minimum(f, 0.0) - jnp.log1p(jnp.exp(-jnp.abs(f)))) * LOG2E
    row = lax.broadcasted_iota(jnp.int32, (tm, tm), 0)
    col = lax.broadcasted_iota(jnp.int32, (tm, tm), 1)
    up_to = (col <= row).astype(BF16)
    cf = carry_ref[0:1, :]
    for part in _bf16_terms(log_f, 3):
        cf = cf + _dot(up_to, part)
    carry_ref[...] = jnp.broadcast_to(cf[tm - 1:tm, :], carry_ref.shape)

    hi, mid, lo = (part.astype(F32) for part in _bf16_terms(cf, 3))
    lane = lax.broadcasted_iota(jnp.int32, (1, LANES), 1)
    for hh in range(N_HEADS):
        terms = jnp.where(lane % 3 == 0, hi[:, hh:hh + 1],
                          jnp.where(lane % 3 == 1, mid[:, hh:hh + 1], lo[:, hh:hh + 1]))
        q_ext = jnp.where(lane < 3, terms, jnp.where(lane < 6, 1.0, 0.0))
        k_ext = jnp.where(lane < 3, 1.0, jnp.where(lane < 6, -terms, 0.0))
        qcat_ref[hh, :, LANES:] = q_ext.astype(BF16)
        kcat_ref[hh, :, LANES:] = k_ext.astype(BF16)

    step = 8
    for c0 in range(0, 3 * N_HEADS, step):
        acc = _dot(h, w_ref[:, c0 * LANES:(c0 + step) * LANES])
        for c in range(step):
            piece = acc[:, c * LANES:(c + 1) * LANES]
            g_idx = c0 + c
            if g_idx < N_HEADS:
                qcat_ref[g_idx, :, :LANES] = (piece * (HEAD_DIM ** -0.5 * LOG2E)).astype(BF16)
            elif g_idx < 2 * N_HEADS:
                kcat_ref[g_idx - N_HEADS, :, :LANES] = piece.astype(BF16)
            else:
                gate_ref[g_idx - 2 * N_HEADS] = _silu(piece)
    for c0 in range(0, N_HEADS, step):
        vt = _dot_nt(wvt_ref[c0 * LANES:(c0 + step) * LANES, :], h)
        for c in range(step):
            vt_ref[c0 + c] = vt[c * LANES:(c + 1) * LANES, :].astype(BF16)


def _forget_proj(x2d, g, w_in, b_f, tokens_per_batch):
    m, d = x2d.shape
    tm = min(PROJ_ROWS, tokens_per_batch)
    di = D_INNER
    w = jnp.concatenate([w_in[:, :2 * di], w_in[:, 3 * di:4 * di]], axis=1).astype(BF16)
    wvt = w_in[:, 2 * di:3 * di].T.astype(BF16)
    wf = jnp.pad(w_in[:, 4 * di:], ((0, 0), (0, LANES - N_HEADS))).astype(BF16)
    bf = jnp.pad(b_f, (0, LANES - N_HEADS)).reshape(1, LANES)
    return pl.pallas_call(
        functools.partial(_forget_proj_kernel, per_batch=tokens_per_batch // tm),
        grid=(m // tm,),
        in_specs=[pl.BlockSpec((tm, d), lambda i: (i, 0)), _resident((1, d)), _resident(w.shape),
                  _resident(wvt.shape), _resident(wf.shape), _resident(bf.shape)],
        out_specs=[_head_block(tm, 2 * LANES), _head_block(tm, 2 * LANES),
                   pl.BlockSpec((N_HEADS, LANES, tm), lambda i: (0, 0, i)), _head_block(tm, LANES)],
        out_shape=[_head_major(m, 2 * LANES, BF16), _head_major(m, 2 * LANES, BF16),
                   jax.ShapeDtypeStruct((N_HEADS, LANES, m), BF16), _head_major(m, LANES, F32)],
        scratch_shapes=[pltpu.VMEM((8, LANES), F32)],
        compiler_params=_params("arbitrary"),
        name="forget_proj",
    )(x2d, g.reshape(1, d), w, wvt, wf, bf)


def _mla_proj_kernel(x_ref, pos_ref, inv_ref, g_ref, w1_ref, qn_ref, wq_ref, kvn_ref, wk_ref, wvt_ref,
                     qcat_ref, kcat_ref, vt_ref, gate_ref):
    h = _rmsnorm(x_ref[...], g_ref[...]).astype(BF16)
    lat = _dot(h, w1_ref[:, :5 * LANES])
    q_lat = lat[:, :MLA_Q_RANK]
    kv_lat = lat[:, MLA_Q_RANK:MLA_Q_RANK + MLA_KV_RANK]
    k_a = lat[:, 3 * LANES:4 * LANES]
    k_b = lat[:, 4 * LANES:5 * LANES]

    lane = lax.broadcasted_iota(jnp.int32, (1, LANES), 1)
    half = MLA_ROPE_DIM // 2
    ang = pos_ref[...] * inv_ref[...]
    cos = jnp.cos(ang)
    sin = jnp.where((lane // half) % 2 == 0, -1.0, 1.0) * jnp.sin(ang)
    k_rope = (k_a * cos + k_b * sin).astype(BF16)

    kvn = _rmsnorm(kv_lat, kvn_ref[...]).astype(BF16)
    step = 8
    for c0 in range(0, N_HEADS, step):
        k_nope = _dot(kvn, wk_ref[:, c0 * LANES:(c0 + step) * LANES])
        vt = _dot_nt(wvt_ref[c0 * LANES:(c0 + step) * LANES, :], kvn)
        for c in range(step):
            kcat_ref[c0 + c] = jnp.concatenate(
                [k_nope[:, c * LANES:(c + 1) * LANES].astype(BF16), k_rope], axis=1)
            vt_ref[c0 + c] = vt[c * LANES:(c + 1) * LANES, :].astype(BF16)

    qn = _rmsnorm(q_lat, qn_ref[...]).astype(BF16)
    n_pairs = N_HEADS // 2
    rope_a = _dot(qn, wq_ref[:, N_HEADS * LANES:(N_HEADS + n_pairs) * LANES])
    rope_b = _dot(qn, wq_ref[:, (N_HEADS + n_pairs) * LANES:])
    first_head = lane < MLA_ROPE_DIM
    q_scale = (MLA_NOPE_DIM + MLA_ROPE_DIM) ** -0.5 * LOG2E
    for c0 in range(0, N_HEADS, step):
        nope = _dot(qn, wq_ref[:, c0 * LANES:(c0 + step) * LANES])
        for c in range(step):
            hh = c0 + c
            pair = hh // 2
            sl = slice(pair * LANES, (pair + 1) * LANES)
            q_rope = rope_a[:, sl] * cos + rope_b[:, sl] * sin
            mine = first_head if hh % 2 == 0 else jnp.logical_not(first_head)
            q_rope = jnp.where(mine, q_rope * q_scale, 0.0)
            q_nope = nope[:, c * LANES:(c + 1) * LANES] * q_scale
            qcat_ref[hh] = jnp.concatenate([q_nope.astype(BF16), q_rope.astype(BF16)], axis=1)

    for c0 in range(0, N_HEADS, step):
        gt = _dot(h, w1_ref[:, (5 + c0) * LANES:(5 + c0 + step) * LANES])
        for c in range(step):
            gate_ref[c0 + c] = _silu(gt[:, c * LANES:(c + 1) * LANES])


def _mla_weights(w_in, w_qb, w_kvb):
    i1 = MLA_Q_RANK
    i2 = i1 + MLA_KV_RANK
    i3 = i2 + MLA_ROPE_DIM
    half = MLA_ROPE_DIM // 2
    wk1, wk2 = w_in[:, i2:i2 + half], w_in[:, i2 + half:i3]
    w1 = jnp.concatenate([w_in[:, :i2], wk1, wk2, wk1, wk2, wk2, wk1, wk2, wk1, w_in[:, i3:]], axis=1)
    wq = w_qb.reshape(MLA_Q_RANK, N_HEADS, MLA_NOPE_DIM + MLA_ROPE_DIM)
    nope = wq[:, :, :MLA_NOPE_DIM].reshape(MLA_Q_RANK, -1)
    r1 = wq[:, :, MLA_NOPE_DIM:MLA_NOPE_DIM + half]
    r2 = wq[:, :, MLA_NOPE_DIM + half:]
    rope_a = jnp.concatenate([r1, r2], axis=2).reshape(MLA_Q_RANK, -1)
    rope_b = jnp.concatenate([r2, r1], axis=2).reshape(MLA_Q_RANK, -1)
    wq2 = jnp.concatenate([nope, rope_a, rope_b], axis=1)
    wkv = w_kvb.reshape(MLA_KV_RANK, N_HEADS, 2 * LANES)
    wk = wkv[:, :, :MLA_NOPE_DIM].reshape(MLA_KV_RANK, -1)
    wvt = wkv[:, :, MLA_NOPE_DIM:].reshape(MLA_KV_RANK, -1).T
    return w1.astype(BF16), wq2.astype(BF16), wk.astype(BF16), wvt.astype(BF16)


def _mla_proj(x2d, positions, ln, w_in, q_norm, w_qb, kv_norm, w_kvb, tokens_per_batch):
    m, d = x2d.shape
    tm = min(PROJ_ROWS, tokens_per_batch)
    w1, wq2, wk, wvt = _mla_weights(w_in, w_qb, w_kvb)
    half = MLA_ROPE_DIM // 2
    inv_freq = ROPE_BASE ** (-jnp.arange(0, MLA_ROPE_DIM, 2, dtype=F32) / MLA_ROPE_DIM)
    inv4 = jnp.tile(inv_freq, LANES // half).reshape(1, LANES)
    pos = positions.astype(F32).reshape(m, 1)
    return pl.pallas_call(
        _mla_proj_kernel,
        grid=(m // tm,),
        in_specs=[pl.BlockSpec((tm, d), lambda i: (i, 0)),
                  pl.BlockSpec((tm, 1), lambda i: (i, 0)),
                  _resident((1, LANES)), _resident((1, d)), _resident(w1.shape),
                  _resident((1, MLA_Q_RANK)), _resident(wq2.shape),
                  _resident((1, MLA_KV_RANK)), _resident(wk.shape), _resident(wvt.shape)],
        out_specs=[_head_block(tm, 2 * LANES), _head_block(tm, 2 * LANES),
                   pl.BlockSpec((N_HEADS, LANES, tm), lambda i: (0, 0, i)), _head_block(tm, LANES)],
        out_shape=[_head_major(m, 2 * LANES, BF16), _head_major(m, 2 * LANES, BF16),
                   jax.ShapeDtypeStruct((N_HEADS, LANES, m), BF16), _head_major(m, LANES, F32)],
        compiler_params=_params("parallel"),
        name="mla_proj",
    )(x2d, pos, inv4, ln.reshape(1, d), w1, q_norm.reshape(1, -1), wq2, kv_norm.reshape(1, -1), wk, wvt)


def _out_kernel(o_ref, w_ref, x_ref, *rest, final):
    if final:
        g_ref, y_ref = rest
    else:
        (y_ref,) = rest
    o = jnp.concatenate([o_ref[hh] for hh in range(N_HEADS)], axis=1)
    y = x_ref[...] + _dot(o, w_ref[...])
    if final:
        y = _rmsnorm(y, g_ref[...])
    y_ref[...] = y


def _out_proj(o, w_out, x2d, final_g=None):
    m, d = x2d.shape
    tm = min(OUT_ROWS, m)
    final = final_g is not None
    in_specs = [_head_block(tm, LANES), _resident((D_INNER, d)), pl.BlockSpec((tm, d), lambda i: (i, 0))]
    args = [o, w_out, x2d]
    if final:
        in_specs.append(_resident((1, d)))
        args.append(final_g.reshape(1, d))
    return pl.pallas_call(
        functools.partial(_out_kernel, final=final),
        grid=(m // tm,),
        in_specs=in_specs,
        out_specs=pl.BlockSpec((tm, d), lambda i: (i, 0)),
        out_shape=jax.ShapeDtypeStruct((m, d), F32),
        compiler_params=_params("parallel"),
        name="out_proj_final" if final else "out_proj",
    )(*args)


def _stick_kernel(q_ref, k_ref, v_ref, g_ref, suffix_ref, o_ref, acc_ref, c_ref, *, t):
    i = pl.program_id(2)
    n_heads = q_ref.shape[0]
    row = lax.broadcasted_iota(jnp.int32, (t, t), 0)
    col = lax.broadcasted_iota(jnp.int32, (t, t), 1)
    strict = col < row

    def keys(kj):
        return pl.ds(pl.multiple_of(kj * t, t), t)

    def skip_bits(z):
        return jnp.maximum(z, 0.0) + jnp.log2(1.0 + jnp.exp2(-jnp.abs(z)))

    def suffix_sums(u):
        return _dot(jnp.concatenate(_bf16_terms(u, 2), axis=1), suffix_ref[...])

    def row_sum(u):
        return jnp.sum(u, axis=1, keepdims=True)

    prev = jnp.maximum(i - 1, 0)
    heads = range(n_heads)
    top = (3 * t) // 4
    half = t // 2
    z_diag, z_prev, u_diag, u_prev = [], [], [], []
    for hh in heads:
        z_diag.append((_dot_nt(q_ref[hh, :half, :], k_ref[hh, pl.ds(pl.multiple_of(i * t, t), half), :]),
                       _dot_nt(q_ref[hh, half:, :], k_ref[hh, keys(i), :])))
        z_prev.append(_dot_nt(q_ref[hh, :top, :], k_ref[hh, keys(prev), :]))
    for hh in heads:
        u_diag.append((jnp.where(strict[:half, :half], skip_bits(z_diag[hh][0]), 0.0),
                       jnp.where(strict[half:], skip_bits(z_diag[hh][1]), 0.0)))
        u_prev.append(skip_bits(z_prev[hh]))
    corner = jnp.concatenate([suffix_ref[:half, :half], suffix_ref[t:t + half, :half]], axis=0)
    later_diag = [(_dot(jnp.concatenate(_bf16_terms(u_diag[hh][0], 2), axis=1), corner),
                   suffix_sums(u_diag[hh][1])) for hh in heads]
    later_prev = [suffix_sums(u_prev[hh]) for hh in heads]
    top_max, rest_max = [], []
    for hh in heads:
        w_hi = jnp.where(strict[:half, :half], jnp.exp2(z_diag[hh][0] - later_diag[hh][0]), 0.0)
        w_lo = jnp.where(strict[half:], jnp.exp2(z_diag[hh][1] - later_diag[hh][1]), 0.0)
        acc = jnp.concatenate(
            [_dot(w_hi.astype(BF16), v_ref[hh, pl.ds(pl.multiple_of(i * t, t), half), :]),
             _dot(w_lo.astype(BF16), v_ref[hh, keys(i), :])], axis=0)
        c = -jnp.concatenate([row_sum(u_diag[hh][0]), row_sum(u_diag[hh][1])], axis=0)
        w = jnp.exp2(z_prev[hh] - later_prev[hh] + c[:top])
        vb = v_ref[hh, keys(prev), :]
        c_top = c[:top] - row_sum(u_prev[hh])
        acc_ref[hh, :top] = acc[:top] + _dot(w.astype(BF16), jnp.where(i > 0, vb, jnp.zeros_like(vb)))
        acc_ref[hh, top:] = acc[top:]
        c_ref[hh, :top] = c_top
        c_ref[hh, top:] = c[top:]
        top_max.append(jnp.max(c_top))
        rest_max.append(jnp.max(c[top:]))
    rest_max = functools.reduce(jnp.maximum, rest_max)
    c_max = jnp.maximum(functools.reduce(jnp.maximum, top_max), rest_max)
    start = jnp.where(rest_max > -UNDERFLOW_LOG2, i - 1, i - 2)
    below_top = lax.broadcasted_iota(jnp.int32, (t, 1), 0) >= top

    def more(carry):
        kj, c_max = carry
        return jnp.logical_and(kj >= 0, c_max > -UNDERFLOW_LOG2)

    def body(carry):
        kj, _ = carry
        todo = jnp.logical_or(kj != i - 1, below_top)
        c_maxes = []
        for hh in heads:
            z = _dot_nt(q_ref[hh], k_ref[hh, keys(kj), :])
            u = jnp.where(todo, skip_bits(z), 0.0)
            w = jnp.where(todo, jnp.exp2(z - suffix_sums(u) + c_ref[hh]), 0.0)
            acc_ref[hh] += _dot(w.astype(BF16), v_ref[hh, keys(kj), :])
            c = c_ref[hh] - row_sum(u)
            c_ref[hh] = c
            c_maxes.append(jnp.max(c))
        return kj - 1, functools.reduce(jnp.maximum, c_maxes)

    lax.while_loop(more, body, (start, c_max))
    for hh in heads:
        o_ref[hh] = (acc_ref[hh] * g_ref[hh]).astype(BF16)


def _stick_attention(qkv, gate, batch, seq):
    t = min(STICK_BLOCK, seq)
    nq = seq // t
    m = batch * seq
    h = N_HEADS
    hp = STICK_HEADS_PER_STEP
    groups = h // hp
    from_here = jnp.tril(jnp.ones((t, t), BF16))
    suffix_sum = jnp.concatenate([from_here, from_here], axis=0)
    return pl.pallas_call(
        functools.partial(_stick_kernel, t=t),
        grid=(groups, batch, nq),
        in_specs=[
            pl.BlockSpec((hp, t, LANES), lambda hh, b, i: (hh, b * nq + i, 0)),
            pl.BlockSpec((hp, seq, LANES), lambda hh, b, i: (groups + hh, b, 0)),
            pl.BlockSpec((hp, seq, LANES), lambda hh, b, i: (2 * groups + hh, b, 0)),
            pl.BlockSpec((hp, t, LANES), lambda hh, b, i: (hh, b * nq + i, 0)),
            _resident(suffix_sum.shape),
        ],
        out_specs=pl.BlockSpec((hp, t, LANES), lambda hh, b, i: (hh, b * nq + i, 0)),
        out_shape=jax.ShapeDtypeStruct((h, m, LANES), BF16),
        scratch_shapes=[pltpu.VMEM((hp, t, HEAD_DIM), F32), pltpu.VMEM((hp, t, 1), F32)],
        compiler_params=_params("parallel", "parallel", "arbitrary"),
        name="stick_attention",
    )(qkv, qkv, qkv, gate, suffix_sum)


def _softmax_kernel(q_ref, k_ref, vt_ref, g_ref, o_ref, s_ref, p_ref, *, t, chunked):
    n_heads, seq, _ = q_ref.shape
    heads = range(n_heads)
    key = lax.broadcasted_iota(jnp.int32, (KEY_SUB, KEY_SUB), 0)
    qry = lax.broadcasted_iota(jnp.int32, (KEY_SUB, KEY_SUB), 1)
    on_diagonal = (key // CHUNK) <= (qry // CHUNK) if chunked else key <= qry

    subs = t // KEY_SUB

    def key_slice(g):
        if isinstance(g, int):
            return pl.ds(g * KEY_SUB, KEY_SUB)
        return pl.ds(pl.multiple_of(g * KEY_SUB, KEY_SUB), KEY_SUB)

    def softmax_step(hh, slot, diag_chunk, pv_prev, m_run, l_run, alpha_prev, acc):
        if diag_chunk is None:
            m_new = jnp.maximum(m_run, jnp.max(s_ref[hh, slot], axis=0, keepdims=True))
            p = jnp.exp2(s_ref[hh, slot] - m_new)
        else:
            lanes = [slice(c * KEY_SUB, (c + 1) * KEY_SUB) for c in range(subs)]
            pieces = []
            for c in range(subs):
                if c < diag_chunk:
                    pieces.append(jnp.full((KEY_SUB, KEY_SUB), NEG, F32))
                elif c == diag_chunk:
                    pieces.append(jnp.where(on_diagonal, s_ref[hh, slot, :, lanes[c]], NEG))
                else:
                    pieces.append(s_ref[hh, slot, :, lanes[c]])
            s_max = jnp.max(jnp.concatenate(pieces, axis=1), axis=0, keepdims=True)
            m_new = jnp.maximum(m_run, s_max)
            p = jnp.concatenate(
                [jnp.zeros((KEY_SUB, KEY_SUB), F32) if c < diag_chunk
                 else jnp.exp2(pieces[c] - m_new[:, lanes[c]]) for c in range(subs)], axis=1)
        alpha = jnp.exp2(m_run - m_new)
        p_ref[hh, slot] = p.astype(BF16)
        l_run = alpha * l_run + jnp.sum(p, axis=0, keepdims=True)
        if pv_prev is not None:
            acc = alpha_prev * acc + pv_prev
        return m_new, l_run, alpha, acc

    n_tiles = seq // t

    def scores(hh, qi, g, slot):
        rows = slice(qi * t, (qi + 1) * t)
        s_ref[hh, slot] = _dot_nt(k_ref[hh, key_slice(g), :], q_ref[hh, rows, :])

    def weighted_values(hh, g, slot):
        return _dot(vt_ref[hh, :, key_slice(g)], p_ref[hh, slot])

    for hh in heads:
        scores(hh, 0, subs - 1, 0)

    for qi in range(n_tiles):
        rows = slice(qi * t, (qi + 1) * t)
        top = (qi + 1) * subs - 1

        def tile(first, states, diagonal, last):
            for j in range(subs):
                g, slot = first - j, j % 2
                diag_chunk = subs - 1 - j if diagonal else None
                for hh in heads:
                    if not (last and j == subs - 1):
                        scores(hh, qi, g - 1, 1 - slot)
                    elif qi + 1 < n_tiles:
                        scores(hh, qi + 1, top + subs, 1 - slot)
                if diagonal and j == 0:
                    pv_prev = [None] * n_heads
                else:
                    pv_prev = [weighted_values(hh, g + 1, 1 - slot) for hh in heads]
                states = tuple(softmax_step(hh, slot, diag_chunk, pv_prev[hh], *states[hh]) for hh in heads)
            return states

        init = (jnp.full((1, t), NEG, F32), jnp.zeros((1, t), F32), jnp.ones((1, t), F32),
                jnp.zeros((HEAD_DIM, t), F32))
        states = tile(top, (init,) * n_heads, True, qi == 0)
        if qi > 1:
            states = lax.fori_loop(
                0, qi - 1, lambda n, st: tile(top - subs * (n + 1), st, False, False), states)
        if qi > 0:
            states = tile(subs - 1, states, False, True)
        for hh in heads:
            _, l_run, alpha, acc = states[hh]
            acc = alpha * acc + weighted_values(hh, 0, 1)
            o_ref[hh, rows, :] = ((acc * (1.0 / l_run)).T * g_ref[hh, rows, :]).astype(BF16)


def _softmax_attention(qcat, kcat, vt, gate, batch, seq, chunked):
    t = min(SOFTMAX_TILE, seq)
    m = batch * seq
    h = N_HEADS
    width = qcat.shape[-1]
    hp = HEADS_PER_STEP
    tokens = lambda w: pl.BlockSpec((hp, seq, w), lambda hh, b: (hh, b, 0))
    return pl.pallas_call(
        functools.partial(_softmax_kernel, t=t, chunked=chunked),
        grid=(h // hp, batch),
        in_specs=[tokens(width), tokens(width),
                  pl.BlockSpec((hp, LANES, seq), lambda hh, b: (hh, 0, b)), tokens(LANES)],
        out_specs=tokens(LANES),
        scratch_shapes=[pltpu.VMEM((hp, 2, min(KEY_SUB, t), t), F32),
                        pltpu.VMEM((hp, 2, min(KEY_SUB, t), t), BF16)],
        out_shape=jax.ShapeDtypeStruct((h, m, LANES), BF16),
        compiler_params=_params("parallel", "parallel"),
        name="mla_attention" if chunked else "forget_attention",
    )(qcat, kcat, vt, gate)


def kernel(x, positions, ln0, w_in0, w_out0, ln1, w_in1, q_norm1, w_qb1, kv_norm1, w_kvb1, w_out1,
           ln2, w_in2, b_f2, w_out2, ln3, w_in3, w_out3, final_norm):
    batch, seq, d = x.shape
    x2d = x.reshape(batch * seq, d)

    qkv, gate = _norm_proj(x2d, ln0, w_in0.astype(BF16), seq)
    o = _stick_attention(qkv, gate, batch, seq)
    x2d = _out_proj(o, w_out0.astype(BF16), x2d)

    qcat, kcat, vt, gate = _mla_proj(x2d, positions, ln1, w_in1, q_norm1, w_qb1, kv_norm1, w_kvb1, seq)
    o = _softmax_attention(qcat, kcat, vt, gate, batch, seq, chunked=True)
    x2d = _out_proj(o, w_out1.astype(BF16), x2d)

    qcat, kcat, vt, gate = _forget_proj(x2d, ln2, w_in2, b_f2, seq)
    o = _softmax_attention(qcat, kcat, vt, gate, batch, seq, chunked=False)
    x2d = _out_proj(o, w_out2.astype(BF16), x2d)

    qkv, gate = _norm_proj(x2d, ln3, w_in3.astype(BF16), seq)
    o = _stick_attention(qkv, gate, batch, seq)
    x2d = _out_proj(o, w_out3.astype(BF16), x2d, final_g=final_norm)
    return x2d.reshape(batch, seq, d)
```
